```python
import math
import jax
import jax.numpy as jnp
from jax import lax
import numpy as np

D_MODEL = 4096
BATCH = 2
SEQ = 4096
DEPTH = 4

HD_A = 64
H_A = D_MODEL // 128
H_KV = H_A // 8
GROUP = H_A // H_KV
WIDTH_A = H_A * HD_A
WINDOW = 128
BLK = WINDOW
H_B = D_MODEL // 256
DK = 128
DV = 128
WIDTH_B = H_B * DV
CONV_K = 4
CHUNK = 64
C_CONV = 2 * H_B * DK + H_B * DV
EPS = 1e-6

SPLIT_SIZES = (WIDTH_A, H_KV * HD_A, H_KV * HD_A, WIDTH_A, C_CONV, WIDTH_B, H_B, H_B, D_MODEL, D_MODEL)
N_IN = 2 * WIDTH_A + 2 * H_KV * HD_A + C_CONV + WIDTH_B + 2 * H_B + 2 * D_MODEL

kernel_name = 'hybrid_swa_sink_alibi_gated_deltanet_gated_merge'


def rms_norm(x, g):
    xf = x.astype(jnp.float32)
    y = xf * lax.rsqrt(jnp.mean(xf * xf, axis=-1, keepdims=True) + EPS)
    return (y * g.astype(jnp.float32)).astype(x.dtype)


def l2_normalize(x):
    xf = x.astype(jnp.float32)
    return xf * lax.rsqrt(jnp.sum(xf * xf, axis=-1, keepdims=True) + EPS)


def alibi_slopes():
    return jnp.exp2(-8.0 * jnp.arange(1, H_A + 1, dtype=jnp.float32) / H_A)


def sliding_window_attention(q, k, v, sinks):
    b, t = q.shape[0], q.shape[1]
    nb = t // BLK
    qb = q.reshape(b, nb, BLK, H_KV, GROUP, HD_A)
    kb = k.reshape(b, nb, BLK, H_KV, HD_A)
    vb = v.reshape(b, nb, BLK, H_KV, HD_A)
    pad = ((0, 0), (1, 0), (0, 0), (0, 0), (0, 0))
    k_band = jnp.concatenate([jnp.pad(kb, pad)[:, :-1], kb], axis=2)
    v_band = jnp.concatenate([jnp.pad(vb, pad)[:, :-1], vb], axis=2)
    s = jnp.einsum('bnqhgd,bnkhd->bnhgqk', qb, k_band).astype(jnp.float32) * (HD_A ** -0.5)
    qi = jnp.arange(BLK)[:, None]
    kj = jnp.arange(2 * BLK)[None, :]
    dist = qi + BLK - kj
    key_pos = jnp.arange(nb)[:, None, None] * BLK - BLK + kj[None]
    valid = (dist >= 0)[None] & (dist < WINDOW)[None] & (key_pos >= 0)
    slopes = alibi_slopes().reshape(H_KV, GROUP)[:, :, None, None]
    s = s - slopes * dist.astype(jnp.float32)
    s = jnp.where(valid[None, :, None, None], s, -jnp.inf)
    sink = jnp.broadcast_to(sinks.astype(jnp.float32).reshape(H_KV, GROUP, 1, 1), s.shape[:-1] + (1,))
    p = jax.nn.softmax(jnp.concatenate([s, sink], axis=-1), axis=-1)[..., :-1]
    o = jnp.einsum('bnhgqk,bnkhd->bnqhgd', p.astype(v.dtype), v_band)
    return o.reshape(b, t, WIDTH_A)


def causal_depthwise_conv(x, w):
    c = x.shape[-1]
    return lax.conv_general_dilated(
        x, w[:, None, :].astype(x.dtype), window_strides=(1,), padding=((CONV_K - 1, 0),),
        dimension_numbers=('NWC', 'WIO', 'NWC'), feature_group_count=c)


def gated_delta_rule(q, k, v, beta, g):
    b, t = q.shape[0], q.shape[1]
    n = t // CHUNK
    f32 = jnp.float32

    def chunks(a):
        a = a.astype(f32).reshape((b, n, CHUNK) + a.shape[2:])
        return jnp.moveaxis(a, 3, 1)

    q, k, v, beta, g = chunks(q), chunks(k), chunks(v), chunks(beta), chunks(g)
    q = q * (DK ** -0.5)
    gc = jnp.cumsum(g, axis=-1)
    causal = jnp.tril(jnp.ones((CHUNK, CHUNK), dtype=bool))
    strict = jnp.tril(jnp.ones((CHUNK, CHUNK), dtype=bool), -1)
    decay = jnp.exp(jnp.where(causal, gc[..., :, None] - gc[..., None, :], -jnp.inf))
    k_beta = k * beta[..., None]
    a_mat = jnp.where(strict, jnp.einsum('bhncd,bhnsd->bhncs', k_beta, k) * decay, 0.0)
    rhs = jnp.concatenate([v * beta[..., None], k_beta * jnp.exp(gc)[..., None]], axis=-1)
    sol = lax.linalg.triangular_solve(a_mat, rhs, left_side=True, lower=True, unit_diagonal=True)
    u, w = sol[..., :DV], sol[..., DV:]
    attn = jnp.einsum('bhncd,bhnsd->bhncs', q, k) * decay
    q_dec = q * jnp.exp(gc)[..., None]
    k_tail = k * jnp.exp(gc[..., -1:] - gc)[..., None]
    chunk_decay = jnp.exp(gc[..., -1])

    def step(state, inp):
        u_i, w_i, a_i, qd_i, kt_i, dec_i = inp
        v_new = u_i - jnp.einsum('bhcd,bhde->bhce', w_i, state)
        o_i = jnp.einsum('bhcd,bhde->bhce', qd_i, state) + jnp.einsum('bhcs,bhse->bhce', a_i, v_new)
        state = state * dec_i[..., None, None] + jnp.einsum('bhcd,bhce->bhde', kt_i, v_new)
        return state, o_i

    xs = tuple(jnp.moveaxis(a, 2, 0) for a in (u, w, attn, q_dec, k_tail, chunk_decay))
    s0 = jnp.zeros((b, H_B, DK, DV), f32)
    _, o = lax.scan(step, s0, xs)
    o = jnp.moveaxis(o, 0, 2)
    return jnp.moveaxis(o, 1, 3).reshape(b, t, H_B, DV)


def setup_inputs(seed: int = 0) -> dict:
    key = jax.random.key(seed)
    ks = jax.random.split(key, 13)
    f32 = jnp.float32
    x = jax.random.normal(ks[0], (BATCH, SEQ, D_MODEL), f32)
    pre_norm_g = 1.0 + 0.02 * jax.random.normal(ks[1], (DEPTH, D_MODEL), f32)
    w_in = jax.random.normal(ks[2], (DEPTH, D_MODEL, N_IN), f32) * (D_MODEL ** -0.5)
    sinks = jax.random.normal(ks[3], (DEPTH, H_A), f32)
    conv_w = jax.random.normal(ks[4], (DEPTH, CONV_K, C_CONV), f32) * (CONV_K ** -0.5)
    a_log = jnp.log(jax.random.uniform(ks[5], (DEPTH, H_B), f32, 1.0, 16.0))
    dt = jnp.exp(jax.random.uniform(ks[6], (DEPTH, H_B), f32, math.log(1e-3), math.log(1e-1)))
    dt_bias = dt + jnp.log(-jnp.expm1(-dt))
    gdn_norm_g = 1.0 + 0.02 * jax.random.normal(ks[7], (DEPTH, DV), f32)
    w_pa = jax.random.normal(ks[8], (DEPTH, WIDTH_A, D_MODEL), f32) * (WIDTH_A ** -0.5)
    w_pb = jax.random.normal(ks[9], (DEPTH, WIDTH_B, D_MODEL), f32) * (WIDTH_B ** -0.5)
    w_o = jax.random.normal(ks[10], (DEPTH, D_MODEL, D_MODEL), f32) * (D_MODEL ** -0.5)
    post_norm_g = 1.0 + 0.02 * jax.random.normal(ks[11], (DEPTH, D_MODEL), f32)
    return {'x': x, 'pre_norm_g': pre_norm_g, 'w_in': w_in, 'sinks': sinks, 'conv_w': conv_w,
            'a_log': a_log, 'dt_bias': dt_bias, 'gdn_norm_g': gdn_norm_g, 'w_pa': w_pa,
            'w_pb': w_pb, 'w_o': w_o, 'post_norm_g': post_norm_g}


def reference(x, pre_norm_g, w_in, sinks, conv_w, a_log, dt_bias, gdn_norm_g, w_pa, w_pb, w_o, post_norm_g):
    b, t, _ = x.shape
    offsets = np.cumsum(SPLIT_SIZES)[:-1].tolist()
    for layer in range(DEPTH):
        h = rms_norm(x, pre_norm_g[layer])
        proj = jnp.einsum('btd,dn->btn', h, w_in[layer])
        (q_a, k_a, v_a, z_a, qkv_b, z_b, beta_raw, alpha_raw, gate_a, gate_b) = jnp.split(proj, offsets, axis=-1)

        o_a = sliding_window_attention(q_a.reshape(b, t, H_A, HD_A), k_a.reshape(b, t, H_KV, HD_A),
                                       v_a.reshape(b, t, H_KV, HD_A), sinks[layer])
        y_a = jnp.einsum('btc,cd->btd', o_a * jax.nn.silu(z_a), w_pa[layer])

        qkv_b = jax.nn.silu(causal_depthwise_conv(qkv_b, conv_w[layer]))
        q_b, k_b, v_b = jnp.split(qkv_b, [H_B * DK, 2 * H_B * DK], axis=-1)
        q_b = l2_normalize(q_b.reshape(b, t, H_B, DK))
        k_b = l2_normalize(k_b.reshape(b, t, H_B, DK))
        v_b = v_b.reshape(b, t, H_B, DV)
        beta = jax.nn.sigmoid(beta_raw.astype(jnp.float32))
        g = -jnp.exp(a_log[layer].astype(jnp.float32)) * jax.nn.softplus(
            alpha_raw.astype(jnp.float32) + dt_bias[layer].astype(jnp.float32))
        o_b = gated_delta_rule(q_b, k_b, v_b, beta, g)
        o_b = rms_norm(o_b, gdn_norm_g[layer]).astype(x.dtype).reshape(b, t, WIDTH_B)
        y_b = jnp.einsum('btc,cd->btd', o_b * jax.nn.silu(z_b), w_pb[layer])

        merged = jax.nn.sigmoid(gate_a) * y_a + jax.nn.sigmoid(gate_b) * y_b
        out = jnp.einsum('btd,de->bte', merged, w_o[layer])
        x = x + rms_norm(out, post_norm_g[layer])
    return x
```

```python
import functools

import jax
import jax.numpy as jnp
from jax import lax
from jax.experimental import pallas as pl
from jax.experimental.pallas import tpu as pltpu

F32 = jnp.float32
BF16 = jnp.bfloat16

D_MODEL = 4096
HD_A = 64
H_A = 32
H_KV = 4
GROUP = H_A // H_KV
WIDTH_A = H_A * HD_A
WINDOW = 128
H_B = 16
DK = 128
DV = 128
WIDTH_B = H_B * DV
CONV_K = 4
CHUNK = 64
EPS = 1e-6

SLAB = 256
HALO = 8
HEADS_PER_STEP = 4
ALPHA_LANE = H_B

VMEM_LIMIT = 56 * 1024 * 1024

COL_QA = 0
COL_ZA = COL_QA + WIDTH_A
COL_QB = COL_ZA + WIDTH_A
COL_KB = COL_QB + H_B * DK
COL_VB = COL_KB + H_B * DK
COL_ZB = COL_VB + H_B * DV
COL_GA = COL_ZB + WIDTH_B
COL_GB = COL_GA + D_MODEL
N_MAIN = COL_GB + D_MODEL
KV_A = H_KV * HD_A
N_TAIL = 2 * KV_A + 128


def _bdot(a, b):
    return jnp.dot(a.astype(BF16), b.astype(BF16), preferred_element_type=F32)


def _bdot_nt(a, b):
    return lax.dot_general(a.astype(BF16), b.astype(BF16), (((1,), (1,)), ((), ())),
                           preferred_element_type=F32)


def _bdot_tn(a, b):
    return lax.dot_general(a.astype(BF16), b.astype(BF16), (((0,), (0,)), ((), ())),
                           preferred_element_type=F32)


def _silu(x):
    return x * jax.nn.sigmoid(x)


def _rmsnorm_kernel(x_ref, g_ref, o_ref):
    x = x_ref[...]
    ms = jnp.mean(x * x, axis=-1, keepdims=True)
    o_ref[...] = (x * lax.rsqrt(ms + EPS) * g_ref[...]).astype(o_ref.dtype)


def _rmsnorm(x, g, rows):
    m, d = x.shape
    return pl.pallas_call(
        _rmsnorm_kernel,
        grid=(m // rows,),
        in_specs=[pl.BlockSpec((rows, d), lambda i: (i, 0)),
                  pl.BlockSpec((1, d), lambda i: (0, 0))],
        out_specs=pl.BlockSpec((rows, d), lambda i: (i, 0)),
        out_shape=jax.ShapeDtypeStruct((m, d), BF16),
        compiler_params=pltpu.CompilerParams(dimension_semantics=("parallel",),
                                             vmem_limit_bytes=VMEM_LIMIT),
        name="pre_norm",
    )(x, g.reshape(1, d))


def _postnorm_kernel(x_ref, y_ref, g_ref, o_ref):
    y = y_ref[...]
    ms = jnp.mean(y * y, axis=-1, keepdims=True)
    o_ref[...] = x_ref[...] + y * lax.rsqrt(ms + EPS) * g_ref[...]


def _postnorm_residual(x, y, g, rows):
    m, d = x.shape
    return pl.pallas_call(
        _postnorm_kernel,
        grid=(m // rows,),
        in_specs=[pl.BlockSpec((rows, d), lambda i: (i, 0)),
                  pl.BlockSpec((rows, d), lambda i: (i, 0)),
                  pl.BlockSpec((1, d), lambda i: (0, 0))],
        out_specs=pl.BlockSpec((rows, d), lambda i: (i, 0)),
        out_shape=jax.ShapeDtypeStruct((m, d), F32),
        compiler_params=pltpu.CompilerParams(dimension_semantics=("parallel",),
                                             vmem_limit_bytes=VMEM_LIMIT),
        name="post_norm",
    )(x, y, g.reshape(1, d))


def _matmul_kernel(a_ref, b_ref, o_ref):
    o_ref[...] = jnp.dot(a_ref[...], b_ref[...], preferred_element_type=F32).astype(o_ref.dtype)


def _matmul(a, b, tm, tn, out_dtype, name):
    m, k = a.shape
    _, n = b.shape
    return pl.pallas_call(
        _matmul_kernel,
        grid=(m // tm, n // tn),
        in_specs=[pl.BlockSpec((tm, k), lambda i, j: (i, 0)),
                  pl.BlockSpec((k, tn), lambda i, j: (0, j))],
        out_specs=pl.BlockSpec((tm, tn), lambda i, j: (i, j)),
        out_shape=jax.ShapeDtypeStruct((m, n), out_dtype),
        compiler_params=pltpu.CompilerParams(dimension_semantics=("parallel", "parallel"),
                                             vmem_limit_bytes=VMEM_LIMIT),
        name=name,
    )(a, b)


def _attn_kernel(sinks_ref, q_ref, z_ref, kc_ref, kp_ref, vc_ref, vp_ref, o_ref):
    n = pl.program_id(1)
    blk = q_ref.shape[0]
    qi = lax.broadcasted_iota(jnp.int32, (blk, 2 * blk), 0)
    kj = lax.broadcasted_iota(jnp.int32, (blk, 2 * blk), 1)
    dist = qi + blk - kj
    valid = (dist >= 0) & (dist < WINDOW) & ((kj >= blk) | (n > 0))
    distf = dist.astype(F32)
    kband = jnp.concatenate([kp_ref[...], kc_ref[...]], axis=0).astype(BF16)
    vband = jnp.concatenate([vp_ref[...], vc_ref[...]], axis=0).astype(BF16)
    scale = HD_A ** -0.5
    for h in range(H_KV):
        kh = kband[:, h * HD_A:(h + 1) * HD_A]
        vh = vband[:, h * HD_A:(h + 1) * HD_A]
        for g in range(GROUP):
            hh = h * GROUP + g
            cols = slice(hh * HD_A, (hh + 1) * HD_A)
            slope = 2.0 ** (-8.0 * (hh + 1) / H_A)
            s = _bdot_nt(q_ref[:, cols], kh) * scale - slope * distf
            s = jnp.where(valid, s, -1e30)
            sink = sinks_ref[hh]
            m = jnp.maximum(jnp.max(s, axis=-1, keepdims=True), sink)
            p = jnp.exp(s - m)
            denom = jnp.sum(p, axis=-1, keepdims=True) + jnp.exp(sink - m)
            o = _bdot(p, vh) / denom
            o_ref[:, cols] = (o * _silu(z_ref[:, cols])).astype(o_ref.dtype)


def _attention(proj_main, proj_tail, sinks, batch, seq):
    nb = seq // WINDOW
    kv_blk = KV_A // KV_A

    def row(b, n):
        return b * nb + n

    def prev_row(b, n):
        return b * nb + jnp.maximum(n - 1, 0)

    return pl.pallas_call(
        _attn_kernel,
        grid=(batch, nb),
        in_specs=[
            pl.BlockSpec(memory_space=pltpu.SMEM),
            pl.BlockSpec((WINDOW, WIDTH_A), lambda b, n: (row(b, n), COL_QA // WIDTH_A)),
            pl.BlockSpec((WINDOW, WIDTH_A), lambda b, n: (row(b, n), COL_ZA // WIDTH_A)),
            pl.BlockSpec((WINDOW, KV_A), lambda b, n: (row(b, n), 0)),
            pl.BlockSpec((WINDOW, KV_A), lambda b, n: (prev_row(b, n), 0)),
            pl.BlockSpec((WINDOW, KV_A), lambda b, n: (row(b, n), kv_blk)),
            pl.BlockSpec((WINDOW, KV_A), lambda b, n: (prev_row(b, n), kv_blk)),
        ],
        out_specs=pl.BlockSpec((WINDOW, WIDTH_A), lambda b, n: (row(b, n), 0)),
        out_shape=jax.ShapeDtypeStruct((batch * seq, WIDTH_A), BF16),
        compiler_params=pltpu.CompilerParams(dimension_semantics=("parallel", "arbitrary"),
                                             vmem_limit_bytes=VMEM_LIMIT),
        name="swa_attention",
    )(sinks, proj_main, proj_main, proj_tail, proj_tail, proj_tail, proj_tail)


def _softplus(x):
    return jnp.maximum(x, 0.0) + jnp.log1p(jnp.exp(-jnp.abs(x)))


def _chunk_scans(g, axis):
    size = g.shape[axis]
    pos = lax.broadcasted_iota(jnp.int32, g.shape, axis) % CHUNK
    pre = g
    suf = g
    step = 1
    while step < CHUNK:
        pre = pre + jnp.where(pos >= step, pltpu.roll(pre, step, axis=axis), 0.0)
        suf = suf + jnp.where(pos < CHUNK - step, pltpu.roll(suf, size - step, axis=axis), 0.0)
        step *= 2
    return pre, suf - g


def _gate_kernel(ba_ref, alt_ref, alog_row_ref, dtb_row_ref, alog_col_ref, dtb_col_ref,
                 beta_ref, gc_ref, gtail_ref, gcrow_ref):
    ba = ba_ref[...]
    beta_ref[...] = jax.nn.sigmoid(ba)
    g = -jnp.exp(alog_row_ref[...]) * _softplus(ba + dtb_row_ref[...])
    gc, gtail = _chunk_scans(g, 0)
    gc_ref[...] = gc
    gtail_ref[...] = gtail
    g_row = -jnp.exp(alog_col_ref[...]) * _softplus(alt_ref[...] + dtb_col_ref[...])
    gcrow_ref[...] = _chunk_scans(g_row, 1)[0]


def _gates(proj_tail, alpha_t, a_log, dt_bias):
    rows = proj_tail.shape[0]
    lane_pad = (ALPHA_LANE, 128 - ALPHA_LANE - H_B)
    alog_row = jnp.pad(a_log, lane_pad).reshape(1, 128)
    dtb_row = jnp.pad(dt_bias, lane_pad).reshape(1, 128)
    ba_blk = (2 * KV_A) // 128
    col_out = jax.ShapeDtypeStruct((rows, 128), F32)
    col_spec = pl.BlockSpec((SLAB, 128), lambda i: (i, 0))
    small = lambda shape: pl.BlockSpec(shape, lambda i: (0, 0))
    return pl.pallas_call(
        _gate_kernel,
        grid=(rows // SLAB,),
        in_specs=[pl.BlockSpec((SLAB, 128), lambda i: (i, ba_blk)),
                  pl.BlockSpec((H_B, SLAB), lambda i: (0, i)),
                  small((1, 128)), small((1, 128)), small((H_B, 1)), small((H_B, 1))],
        out_specs=[col_spec, col_spec, col_spec, pl.BlockSpec((H_B, SLAB), lambda i: (0, i))],
        out_shape=[col_out, col_out, col_out, jax.ShapeDtypeStruct((H_B, rows), F32)],
        compiler_params=pltpu.CompilerParams(dimension_semantics=("parallel",),
                                             vmem_limit_bytes=VMEM_LIMIT),
        name="gdn_gates",
    )(proj_tail, alpha_t, alog_row, dtb_row, a_log.reshape(H_B, 1), dt_bias.reshape(H_B, 1))


def _unit_lower_inverse(a, same16, same32):
    n = a.shape[0]
    eye = (lax.broadcasted_iota(jnp.int32, (n, n), 0) ==
           lax.broadcasted_iota(jnp.int32, (n, n), 1)).astype(F32)
    d = jnp.where(same16, a, 0.0)
    d2 = _bdot(d, d)
    d4 = _bdot(d2, d2)
    d8 = _bdot(d4, d4)
    t = eye - d
    t = t + _bdot(t, d2)
    t = t + _bdot(t, d4)
    t = t + _bdot(t, d8)
    e = jnp.where(same32 & ~same16, a, 0.0)
    t = t - _bdot(_bdot(t, e), t)
    e = jnp.where(~same32, a, 0.0)
    t = t - _bdot(_bdot(t, e), t)
    return t


def _gdn_kernel(q_ref, k_ref, v_ref, qh_ref, kh_ref, vh_ref, wq_ref, wk_ref, wv_ref,
                beta_ref, gc_ref, gtail_ref, gcrow_ref, z_ref, ng_ref, o_ref,
                state_ref, qx_ref, kx_ref, vx_ref):
    hg = pl.program_id(1)
    t = pl.program_id(2)
    n = q_ref.shape[0]

    @pl.when(t == 0)
    def _():
        state_ref[...] = jnp.zeros_like(state_ref)

    def conv_silu(x_ref, halo_ref, w_ref, xx_ref):
        xx_ref[0:HALO, :] = jnp.where(t > 0, halo_ref[...], 0.0)
        xx_ref[HALO:HALO + n, :] = x_ref[...]
        acc = None
        for j in range(CONV_K):
            start = HALO - (CONV_K - 1) + j
            term = w_ref[j:j + 1, :] * xx_ref[start:start + n, :]
            acc = term if acc is None else acc + term
        return _silu(acc)

    q_all = conv_silu(q_ref, qh_ref, wq_ref, qx_ref)
    k_all = conv_silu(k_ref, kh_ref, wk_ref, kx_ref)
    v_all = conv_silu(v_ref, vh_ref, wv_ref, vx_ref)

    lane = lax.broadcasted_iota(jnp.int32, (n, 128), 1)

    def pick_lane(ref, idx):
        return jnp.sum(jnp.where(lane == idx, ref[...], 0.0), axis=-1, keepdims=True)

    row = lax.broadcasted_iota(jnp.int32, (n, n), 0)
    col = lax.broadcasted_iota(jnp.int32, (n, n), 1)
    same_chunk = (row // CHUNK) == (col // CHUNK)
    causal = same_chunk & (row >= col)
    strict = same_chunk & (row > col)
    same16 = (row // 16) == (col // 16)
    same32 = (row // 32) == (col // 32)

    for hl in range(HEADS_PER_STEP):
        lanes = slice(hl * DK, (hl + 1) * DK)
        head = hg * HEADS_PER_STEP + hl
        q = q_all[:, lanes]
        k = k_all[:, lanes]
        v = v_all[:, lanes]
        q = q * (lax.rsqrt(jnp.sum(q * q, axis=-1, keepdims=True) + EPS) * (DK ** -0.5))
        k = k * lax.rsqrt(jnp.sum(k * k, axis=-1, keepdims=True) + EPS)
        beta = pick_lane(beta_ref, head)
        gc = pick_lane(gc_ref, ALPHA_LANE + head)
        gtail = pick_lane(gtail_ref, ALPHA_LANE + head)
        gcrow = gcrow_ref[pl.ds(head, 1), :]
        decay = jnp.exp(jnp.where(causal, gc - gcrow, -1e30))
        eg = jnp.exp(gc)
        kb = k * beta
        gram = _bdot_nt(jnp.concatenate([kb, q], axis=0), k)
        a = jnp.where(strict, gram[:n] * decay, 0.0)
        attn = gram[n:] * decay
        tinv = _unit_lower_inverse(a, same16, same32)
        sol = _bdot(tinv, jnp.concatenate([v * beta, kb * eg], axis=1))
        u = sol[:, :DV]
        w = sol[:, DV:]
        qd = q * eg
        kt = k * jnp.exp(gtail)
        state = state_ref[hl]
        v_new = []
        o_state = []
        for c in range(n // CHUNK):
            rows = slice(c * CHUNK, (c + 1) * CHUNK)
            ws = _bdot(jnp.concatenate([w[rows], qd[rows]], axis=0), state)
            vn = u[rows] - ws[:CHUNK]
            v_new.append(vn)
            o_state.append(ws[CHUNK:])
            chunk_decay = jnp.exp(gc[(c + 1) * CHUNK - 1:(c + 1) * CHUNK, :])
            state = state * chunk_decay + _bdot_tn(kt[rows], vn)
        state_ref[hl] = state
        o = jnp.concatenate(o_state, axis=0) + _bdot(attn, jnp.concatenate(v_new, axis=0))
        o = o * lax.rsqrt(jnp.mean(o * o, axis=-1, keepdims=True) + EPS) * ng_ref[...]
        o_ref[:, lanes] = (o * _silu(z_ref[:, lanes])).astype(o_ref.dtype)


def _gated_deltanet(proj_main, conv_w, beta, gc, gtail, gcrow, norm_g, batch, seq):
    width = HEADS_PER_STEP * DK
    nt = seq // SLAB
    halo_per_slab = SLAB // HALO

    def rows(b, h, t):
        return b * nt + t

    def halo_rows(b, h, t):
        return jnp.maximum((b * nt + t) * halo_per_slab - 1, 0)

    def sect(col0):
        base = col0 // width
        return (pl.BlockSpec((SLAB, width), lambda b, h, t: (rows(b, h, t), base + h)),
                pl.BlockSpec((HALO, width), lambda b, h, t: (halo_rows(b, h, t), base + h)))

    q_spec, qh_spec = sect(COL_QB)
    k_spec, kh_spec = sect(COL_KB)
    v_spec, vh_spec = sect(COL_VB)
    z_spec, _ = sect(COL_ZB)

    def conv_spec(col0):
        base = col0 // width
        return pl.BlockSpec((CONV_K, width), lambda b, h, t: (0, base + h))

    col_spec = pl.BlockSpec((SLAB, 128), lambda b, h, t: (rows(b, h, t), 0))
    return pl.pallas_call(
        _gdn_kernel,
        grid=(batch, H_B // HEADS_PER_STEP, nt),
        in_specs=[q_spec, k_spec, v_spec, qh_spec, kh_spec, vh_spec,
                  conv_spec(0), conv_spec(H_B * DK), conv_spec(2 * H_B * DK),
                  col_spec, col_spec, col_spec,
                  pl.BlockSpec((H_B, SLAB), lambda b, h, t: (0, rows(b, h, t))),
                  z_spec,
                  pl.BlockSpec((1, DV), lambda b, h, t: (0, 0))],
        out_specs=pl.BlockSpec((SLAB, width), lambda b, h, t: (rows(b, h, t), h)),
        out_shape=jax.ShapeDtypeStruct((batch * seq, WIDTH_B), BF16),
        scratch_shapes=[pltpu.VMEM((HEADS_PER_STEP, DK, DV), F32),
                        pltpu.VMEM((HALO + SLAB, width), F32),
                        pltpu.VMEM((HALO + SLAB, width), F32),
                        pltpu.VMEM((HALO + SLAB, width), F32)],
        compiler_params=pltpu.CompilerParams(
            dimension_semantics=("parallel", "parallel", "arbitrary"),
            vmem_limit_bytes=VMEM_LIMIT),
        name="gated_deltanet",
    )(proj_main, proj_main, proj_main, proj_main, proj_main, proj_main,
      conv_w, conv_w, conv_w, beta, gc, gtail, gcrow, proj_main, norm_g.reshape(1, DV))


def _merge_kernel(oa_ref, ob_ref, wa_ref, wb_ref, ga_ref, gb_ref, o_ref):
    ya = jnp.dot(oa_ref[...], wa_ref[...], preferred_element_type=F32)
    yb = jnp.dot(ob_ref[...], wb_ref[...], preferred_element_type=F32)
    merged = jax.nn.sigmoid(ga_ref[...]) * ya + jax.nn.sigmoid(gb_ref[...]) * yb
    o_ref[...] = merged.astype(o_ref.dtype)


def _merge(o_a, o_b, w_pa, w_pb, proj_main, tm, tn):
    m = o_a.shape[0]
    ga_blk = COL_GA // tn
    gb_blk = COL_GB // tn
    return pl.pallas_call(
        _merge_kernel,
        grid=(m // tm, D_MODEL // tn),
        in_specs=[pl.BlockSpec((tm, WIDTH_A), lambda i, j: (i, 0)),
                  pl.BlockSpec((tm, WIDTH_B), lambda i, j: (i, 0)),
                  pl.BlockSpec((WIDTH_A, tn), lambda i, j: (0, j)),
                  pl.BlockSpec((WIDTH_B, tn), lambda i, j: (0, j)),
                  pl.BlockSpec((tm, tn), lambda i, j: (i, ga_blk + j)),
                  pl.BlockSpec((tm, tn), lambda i, j: (i, gb_blk + j))],
        out_specs=pl.BlockSpec((tm, tn), lambda i, j: (i, j)),
        out_shape=jax.ShapeDtypeStruct((m, D_MODEL), BF16),
        compiler_params=pltpu.CompilerParams(dimension_semantics=("parallel", "parallel"),
                                             vmem_limit_bytes=VMEM_LIMIT),
        name="branch_merge",
    )(o_a, o_b, w_pa, w_pb, proj_main, proj_main)


def _split_w_in(w):
    qa, ka, va, za = 0, WIDTH_A, WIDTH_A + KV_A, WIDTH_A + 2 * KV_A
    qkvb = za + WIDTH_A
    zb = qkvb + 2 * H_B * DK + H_B * DV
    ba = zb + WIDTH_B
    gates = ba + 2 * H_B
    main = jnp.concatenate([w[:, qa:ka], w[:, za:qkvb], w[:, qkvb:ba], w[:, gates:]], axis=1)
    tail = jnp.concatenate([w[:, ka:za], w[:, ba:gates],
                            jnp.zeros((w.shape[0], 128 - 2 * H_B), w.dtype)], axis=1)
    return main.astype(BF16), tail.astype(BF16)


def _layer(x2, batch, seq, pre_g, w_in, sinks, conv_w, a_log, dt_bias, gdn_g, w_pa, w_pb, w_o, post_g):
    w_main, w_tail = _split_w_in(w_in)
    h = _rmsnorm(x2, pre_g, 256)
    proj_main = _matmul(h, w_main, 1024, 1024, F32, "in_proj_main")
    proj_tail = _matmul(h, w_tail, 1024, N_TAIL, F32, "in_proj_tail")
    o_a = _attention(proj_main, proj_tail, sinks, batch, seq)
    alpha_t = proj_tail[:, 2 * KV_A + ALPHA_LANE:2 * KV_A + ALPHA_LANE + H_B].T
    beta, gc, gtail, gcrow = _gates(proj_tail, alpha_t, a_log, dt_bias)
    o_b = _gated_deltanet(proj_main, conv_w, beta, gc, gtail, gcrow, gdn_g, batch, seq)
    merged = _merge(o_a, o_b, w_pa.astype(BF16), w_pb.astype(BF16), proj_main, 1024, 512)
    out = _matmul(merged, w_o.astype(BF16), 1024, 1024, F32, "out_proj")
    return _postnorm_residual(x2, out, post_g, 256)


def kernel(x, pre_norm_g, w_in, sinks, conv_w, a_log, dt_bias, gdn_norm_g, w_pa, w_pb, w_o, post_norm_g):
    batch, seq, d = x.shape
    x2 = x.reshape(batch * seq, d)
    for layer in range(w_in.shape[0]):
        x2 = _layer(x2, batch, seq, pre_norm_g[layer], w_in[layer], sinks[layer], conv_w[layer],
                    a_log[layer], dt_bias[layer], gdn_norm_g[layer], w_pa[layer], w_pb[layer],
                    w_o[layer], post_norm_g[layer])
    return x2.reshape(batch, seq, d)
```

```python
import functools

import jax
import jax.numpy as jnp
from jax import lax
from jax.experimental import pallas as pl
from jax.experimental.pallas import tpu as pltpu

F32 = jnp.float32
BF16 = jnp.bfloat16

D_MODEL = 4096
HD_A = 64
H_A = 32
H_KV = 4
GROUP = H_A // H_KV
WIDTH_A = H_A * HD_A
WINDOW = 128
H_B = 16
DK = 128
DV = 128
WIDTH_B = H_B * DV
CONV_K = 4
CHUNK = 64
EPS = 1e-6

SLAB = 256
HALO = 8
HEADS_PER_STEP = 8
ALPHA_LANE = H_B

VMEM_LIMIT = 56 * 1024 * 1024

COL_QA = 0
COL_ZA = COL_QA + WIDTH_A
COL_QB = COL_ZA + WIDTH_A
COL_KB = COL_QB + H_B * DK
COL_VB = COL_KB + H_B * DK
COL_ZB = COL_VB + H_B * DV
KV_A = H_KV * HD_A
COL_KA = COL_ZB + WIDTH_B
COL_VA = COL_KA + KV_A
COL_BA = COL_VA + KV_A
W_TILE = 512
N_MAIN = COL_BA + W_TILE
SRC_KV_TILE = WIDTH_A // W_TILE
SRC_GATES = 2 * WIDTH_A + 2 * KV_A + 2 * H_B * DK + H_B * DV + WIDTH_B + 2 * H_B


def _bdot(a, b):
    return jnp.dot(a.astype(BF16), b.astype(BF16), preferred_element_type=F32)


def _bdot_nt(a, b):
    return lax.dot_general(a.astype(BF16), b.astype(BF16), (((1,), (1,)), ((), ())),
                           preferred_element_type=F32)


def _bdot_tn(a, b):
    return lax.dot_general(a.astype(BF16), b.astype(BF16), (((0,), (0,)), ((), ())),
                           preferred_element_type=F32)


def _silu(x):
    return x * jax.nn.sigmoid(x)


def _rmsnorm_kernel(x_ref, g_ref, o_ref):
    x = x_ref[...]
    ms = jnp.mean(x * x, axis=-1, keepdims=True)
    o_ref[...] = (x * lax.rsqrt(ms + EPS) * g_ref[...]).astype(o_ref.dtype)


def _rmsnorm(x, g, rows):
    m, d = x.shape
    return pl.pallas_call(
        _rmsnorm_kernel,
        grid=(m // rows,),
        in_specs=[pl.BlockSpec((rows, d), lambda i: (i, 0)),
                  pl.BlockSpec((1, d), lambda i: (0, 0))],
        out_specs=pl.BlockSpec((rows, d), lambda i: (i, 0)),
        out_shape=jax.ShapeDtypeStruct((m, d), BF16),
        compiler_params=pltpu.CompilerParams(dimension_semantics=("parallel",),
                                             vmem_limit_bytes=VMEM_LIMIT),
        name="pre_norm",
    )(x, g.reshape(1, d))


def _postnorm_kernel(x_ref, y_ref, g_ref, o_ref):
    y = y_ref[...]
    ms = jnp.mean(y * y, axis=-1, keepdims=True)
    o_ref[...] = x_ref[...] + y * lax.rsqrt(ms + EPS) * g_ref[...]


def _postnorm_residual(x, y, g, rows):
    m, d = x.shape
    return pl.pallas_call(
        _postnorm_kernel,
        grid=(m // rows,),
        in_specs=[pl.BlockSpec((rows, d), lambda i: (i, 0)),
                  pl.BlockSpec((rows, d), lambda i: (i, 0)),
                  pl.BlockSpec((1, d), lambda i: (0, 0))],
        out_specs=pl.BlockSpec((rows, d), lambda i: (i, 0)),
        out_shape=jax.ShapeDtypeStruct((m, d), F32),
        compiler_params=pltpu.CompilerParams(dimension_semantics=("parallel",),
                                             vmem_limit_bytes=VMEM_LIMIT),
        name="post_norm",
    )(x, y, g.reshape(1, d))


def _matmul_kernel(a_ref, b_ref, o_ref):
    o_ref[...] = jnp.dot(a_ref[...], b_ref[...], preferred_element_type=F32).astype(o_ref.dtype)


def _matmul_f32w_kernel(a_ref, w_ref, o_ref, wb_ref):
    @pl.when(pl.program_id(1) == 0)
    def _():
        wb_ref[...] = w_ref[...].astype(BF16)

    o_ref[...] = jnp.dot(a_ref[...], wb_ref[...], preferred_element_type=F32).astype(o_ref.dtype)


def _matmul_f32w(a, w_stack, layer, n_tiles, out_tile, tm, out_dtype, name):
    m, k = a.shape
    return pl.pallas_call(
        _matmul_f32w_kernel,
        grid=(n_tiles, m // tm),
        in_specs=[pl.BlockSpec((tm, k), lambda j, i: (i, 0)),
                  pl.BlockSpec((None, k, W_TILE), lambda j, i: (layer, 0, j))],
        out_specs=pl.BlockSpec((tm, W_TILE), lambda j, i: (i, out_tile(j))),
        out_shape=jax.ShapeDtypeStruct((m, n_tiles * W_TILE), out_dtype),
        scratch_shapes=[pltpu.VMEM((k, W_TILE), BF16)],
        compiler_params=pltpu.CompilerParams(dimension_semantics=("parallel", "arbitrary"),
                                             vmem_limit_bytes=VMEM_LIMIT),
        name=name,
    )(a, w_stack)


def _matmul(a, b, tm, tn, out_dtype, name):
    m, k = a.shape
    _, n = b.shape
    return pl.pallas_call(
        _matmul_kernel,
        grid=(m // tm, n // tn),
        in_specs=[pl.BlockSpec((tm, k), lambda i, j: (i, 0)),
                  pl.BlockSpec((k, tn), lambda i, j: (0, j))],
        out_specs=pl.BlockSpec((tm, tn), lambda i, j: (i, j)),
        out_shape=jax.ShapeDtypeStruct((m, n), out_dtype),
        compiler_params=pltpu.CompilerParams(dimension_semantics=("parallel", "parallel"),
                                             vmem_limit_bytes=VMEM_LIMIT),
        name=name,
    )(a, b)


def _attn_kernel(sinks_ref, q_ref, z_ref, kc_ref, kp_ref, vc_ref, vp_ref, o_ref):
    n = pl.program_id(1)
    blk = q_ref.shape[0]
    qi = lax.broadcasted_iota(jnp.int32, (blk, 2 * blk), 0)
    kj = lax.broadcasted_iota(jnp.int32, (blk, 2 * blk), 1)
    dist = qi + blk - kj
    valid = (dist >= 0) & (dist < WINDOW) & ((kj >= blk) | (n > 0))
    distf = dist.astype(F32)
    kband = jnp.concatenate([kp_ref[...], kc_ref[...]], axis=0).astype(BF16)
    vband = jnp.concatenate([vp_ref[...], vc_ref[...]], axis=0).astype(BF16)
    scale = HD_A ** -0.5
    for h in range(H_KV):
        kh = kband[:, h * HD_A:(h + 1) * HD_A]
        vh = vband[:, h * HD_A:(h + 1) * HD_A]
        for g in range(GROUP):
            hh = h * GROUP + g
            cols = slice(hh * HD_A, (hh + 1) * HD_A)
            slope = 2.0 ** (-8.0 * (hh + 1) / H_A)
            s = _bdot_nt(q_ref[:, cols], kh) * scale - slope * distf
            s = jnp.where(valid, s, -1e30)
            sink = sinks_ref[hh]
            m = jnp.maximum(jnp.max(s, axis=-1, keepdims=True), sink)
            p = jnp.exp(s - m)
            denom = jnp.sum(p, axis=-1, keepdims=True) + jnp.exp(sink - m)
            o = _bdot(p, vh) / denom
            o_ref[:, cols] = (o * _silu(z_ref[:, cols])).astype(o_ref.dtype)


def _attention(proj, sinks, batch, seq):
    nb = seq // WINDOW
    k_blk = COL_KA // KV_A
    v_blk = COL_VA // KV_A

    def row(b, n):
        return b * nb + n

    def prev_row(b, n):
        return b * nb + jnp.maximum(n - 1, 0)

    return pl.pallas_call(
        _attn_kernel,
        grid=(batch, nb),
        in_specs=[
            pl.BlockSpec(memory_space=pltpu.SMEM),
            pl.BlockSpec((WINDOW, WIDTH_A), lambda b, n: (row(b, n), COL_QA // WIDTH_A)),
            pl.BlockSpec((WINDOW, WIDTH_A), lambda b, n: (row(b, n), COL_ZA // WIDTH_A)),
            pl.BlockSpec((WINDOW, KV_A), lambda b, n: (row(b, n), k_blk)),
            pl.BlockSpec((WINDOW, KV_A), lambda b, n: (prev_row(b, n), k_blk)),
            pl.BlockSpec((WINDOW, KV_A), lambda b, n: (row(b, n), v_blk)),
            pl.BlockSpec((WINDOW, KV_A), lambda b, n: (prev_row(b, n), v_blk)),
        ],
        out_specs=pl.BlockSpec((WINDOW, WIDTH_A), lambda b, n: (row(b, n), 0)),
        out_shape=jax.ShapeDtypeStruct((batch * seq, WIDTH_A), BF16),
        compiler_params=pltpu.CompilerParams(dimension_semantics=("parallel", "arbitrary"),
                                             vmem_limit_bytes=VMEM_LIMIT),
        name="swa_attention",
    )(sinks, proj, proj, proj, proj, proj, proj)


def _softplus(x):
    return jnp.maximum(x, 0.0) + jnp.log1p(jnp.exp(-jnp.abs(x)))


def _chunk_scans(g, axis):
    size = g.shape[axis]
    pos = lax.broadcasted_iota(jnp.int32, g.shape, axis) % CHUNK
    pre = g
    suf = g
    step = 1
    while step < CHUNK:
        pre = pre + jnp.where(pos >= step, pltpu.roll(pre, step, axis=axis), 0.0)
        suf = suf + jnp.where(pos < CHUNK - step, pltpu.roll(suf, size - step, axis=axis), 0.0)
        step *= 2
    return pre, suf - g


def _gate_kernel(ba_ref, alt_ref, alog_row_ref, dtb_row_ref, alog_col_ref, dtb_col_ref,
                 beta_ref, gc_ref, gtail_ref, gcrow_ref):
    ba = ba_ref[...]
    beta_ref[...] = jax.nn.sigmoid(ba)
    g = -jnp.exp(alog_row_ref[...]) * _softplus(ba + dtb_row_ref[...])
    gc, gtail = _chunk_scans(g, 0)
    gc_ref[...] = gc
    gtail_ref[...] = gtail
    g_row = -jnp.exp(alog_col_ref[...]) * _softplus(alt_ref[...] + dtb_col_ref[...])
    gcrow_ref[...] = _chunk_scans(g_row, 1)[0]


def _gates(proj, alpha_t, a_log, dt_bias):
    rows = proj.shape[0]
    lane_pad = (ALPHA_LANE, 128 - ALPHA_LANE - H_B)
    alog_row = jnp.pad(a_log, lane_pad).reshape(1, 128)
    dtb_row = jnp.pad(dt_bias, lane_pad).reshape(1, 128)
    ba_blk = COL_BA // 128
    col_out = jax.ShapeDtypeStruct((rows, 128), F32)
    col_spec = pl.BlockSpec((SLAB, 128), lambda i: (i, 0))
    small = lambda shape: pl.BlockSpec(shape, lambda i: (0, 0))
    return pl.pallas_call(
        _gate_kernel,
        grid=(rows // SLAB,),
        in_specs=[pl.BlockSpec((SLAB, 128), lambda i: (i, ba_blk)),
                  pl.BlockSpec((H_B, SLAB), lambda i: (0, i)),
                  small((1, 128)), small((1, 128)), small((H_B, 1)), small((H_B, 1))],
        out_specs=[col_spec, col_spec, col_spec, pl.BlockSpec((H_B, SLAB), lambda i: (0, i))],
        out_shape=[col_out, col_out, col_out, jax.ShapeDtypeStruct((H_B, rows), F32)],
        compiler_params=pltpu.CompilerParams(dimension_semantics=("parallel",),
                                             vmem_limit_bytes=VMEM_LIMIT),
        name="gdn_gates",
    )(proj, alpha_t, alog_row, dtb_row, a_log.reshape(H_B, 1), dt_bias.reshape(H_B, 1))


def _unit_lower_inverse(a_list, same16, same32):
    n = a_list[0].shape[0]
    eye = (lax.broadcasted_iota(jnp.int32, (n, n), 0) ==
           lax.broadcasted_iota(jnp.int32, (n, n), 1)).astype(F32)
    d = [jnp.where(same16, a, 0.0) for a in a_list]
    db = [x.astype(BF16) for x in d]
    d2 = [_bdot(x, x).astype(BF16) for x in db]
    t = [eye - x for x in d]
    d4 = [_bdot(x, x).astype(BF16) for x in d2]
    t = [x + _bdot(x, y) for x, y in zip(t, d2)]
    d8 = [_bdot(x, x) for x in d4]
    t = [x + _bdot(x, y) for x, y in zip(t, d4)]
    t = [x + _bdot(x, y) for x, y in zip(t, d8)]
    for lower, upper in ((same16, same32), (same32, None)):
        off = ~lower if upper is None else upper & ~lower
        e = [jnp.where(off, a, 0.0) for a in a_list]
        tb = [x.astype(BF16) for x in t]
        te = [_bdot(x, y) for x, y in zip(tb, e)]
        t = [x - _bdot(y, z) for x, y, z in zip(t, te, tb)]
    return t


def _gdn_kernel(q_ref, k_ref, v_ref, qh_ref, kh_ref, vh_ref, wq_ref, wk_ref, wv_ref,
                beta_ref, gc_ref, gtail_ref, gcrow_ref, z_ref, ng_ref, o_ref,
                state_ref, qx_ref, kx_ref, vx_ref):
    hg = pl.program_id(1)
    t = pl.program_id(2)
    n = q_ref.shape[0]

    @pl.when(t == 0)
    def _():
        state_ref[...] = jnp.zeros_like(state_ref)

    def conv_silu(x_ref, halo_ref, w_ref, xx_ref):
        xx_ref[0:HALO, :] = jnp.where(t > 0, halo_ref[...], 0.0)
        xx_ref[HALO:HALO + n, :] = x_ref[...]
        acc = None
        for j in range(CONV_K):
            start = HALO - (CONV_K - 1) + j
            term = w_ref[j:j + 1, :] * xx_ref[start:start + n, :]
            acc = term if acc is None else acc + term
        return _silu(acc)

    q_all = conv_silu(q_ref, qh_ref, wq_ref, qx_ref)
    k_all = conv_silu(k_ref, kh_ref, wk_ref, kx_ref)
    v_all = conv_silu(v_ref, vh_ref, wv_ref, vx_ref)

    lane = lax.broadcasted_iota(jnp.int32, (n, 128), 1)

    def pick_lane(ref, idx):
        return jnp.sum(jnp.where(lane == idx, ref[...], 0.0), axis=-1, keepdims=True)

    row = lax.broadcasted_iota(jnp.int32, (n, n), 0)
    col = lax.broadcasted_iota(jnp.int32, (n, n), 1)
    same_chunk = (row // CHUNK) == (col // CHUNK)
    causal = same_chunk & (row >= col)
    strict = same_chunk & (row > col)
    same16 = (row // 16) == (col // 16)
    same32 = (row // 32) == (col // 32)

    heads = range(HEADS_PER_STEP)
    lanes = [slice(hl * DK, (hl + 1) * DK) for hl in heads]
    head = [hg * HEADS_PER_STEP + hl for hl in heads]
    q = [q_all[:, s] for s in lanes]
    k = [k_all[:, s] for s in lanes]
    v = [v_all[:, s] for s in lanes]
    q = [x * (lax.rsqrt(jnp.sum(x * x, axis=-1, keepdims=True) + EPS) * (DK ** -0.5)) for x in q]
    k = [x * lax.rsqrt(jnp.sum(x * x, axis=-1, keepdims=True) + EPS) for x in k]
    beta = [pick_lane(beta_ref, h) for h in head]
    gc = [pick_lane(gc_ref, ALPHA_LANE + h) for h in head]
    gtail = [pick_lane(gtail_ref, ALPHA_LANE + h) for h in head]
    gcrow = [gcrow_ref[pl.ds(h, 1), :] for h in head]
    kb = [x * b for x, b in zip(k, beta)]
    gram = [_bdot_nt(jnp.concatenate([x, y], axis=0), z) for x, y, z in zip(kb, q, k)]
    decay = [jnp.exp(jnp.where(causal, c - r, -1e30)) for c, r in zip(gc, gcrow)]
    a = [jnp.where(strict, g[:n] * d, 0.0) for g, d in zip(gram, decay)]
    attn = [(g[n:] * d).astype(BF16) for g, d in zip(gram, decay)]
    tinv = _unit_lower_inverse(a, same16, same32)
    eg = [jnp.exp(c) for c in gc]
    rhs = [jnp.concatenate([x * b, y * e], axis=1) for x, b, y, e in zip(v, beta, kb, eg)]
    sol = [_bdot(x, y) for x, y in zip(tinv, rhs)]
    u = [x[:, :DV] for x in sol]
    w = [x[:, DV:].astype(BF16) for x in sol]
    qd = [(x * e).astype(BF16) for x, e in zip(q, eg)]
    kt = [(x * jnp.exp(g)).astype(BF16) for x, g in zip(k, gtail)]
    state = [state_ref[hl] for hl in heads]
    v_new = [[] for _ in heads]
    o_state = [[] for _ in heads]
    for c in range(n // CHUNK):
        rows = slice(c * CHUNK, (c + 1) * CHUNK)
        last = (c + 1) * CHUNK - 1
        lhs = [jnp.concatenate([x[rows], y[rows]], axis=0) for x, y in zip(w, qd)]
        ws = [_bdot(x, s) for x, s in zip(lhs, state)]
        vn = [x[rows] - y[:CHUNK] for x, y in zip(u, ws)]
        for hl in heads:
            v_new[hl].append(vn[hl])
            o_state[hl].append(ws[hl][CHUNK:])
        chunk_decay = [jnp.exp(g[last:last + 1, :]) for g in gc]
        state = [s * d + _bdot_tn(x[rows], y) for s, d, x, y in zip(state, chunk_decay, kt, vn)]
    for hl in heads:
        state_ref[hl] = state[hl]
    o = [jnp.concatenate(x, axis=0) + _bdot(y, jnp.concatenate(z, axis=0))
         for x, y, z in zip(o_state, attn, v_new)]
    for hl in heads:
        y = o[hl]
        y = y * lax.rsqrt(jnp.mean(y * y, axis=-1, keepdims=True) + EPS) * ng_ref[...]
        o_ref[:, lanes[hl]] = (y * _silu(z_ref[:, lanes[hl]])).astype(o_ref.dtype)


def _gated_deltanet(proj_main, conv_w, beta, gc, gtail, gcrow, norm_g, batch, seq):
    width = HEADS_PER_STEP * DK
    nt = seq // SLAB
    halo_per_slab = SLAB // HALO

    def rows(b, h, t):
        return b * nt + t

    def halo_rows(b, h, t):
        return jnp.maximum((b * nt + t) * halo_per_slab - 1, 0)

    def sect(col0):
        base = col0 // width
        return (pl.BlockSpec((SLAB, width), lambda b, h, t: (rows(b, h, t), base + h)),
                pl.BlockSpec((HALO, width), lambda b, h, t: (halo_rows(b, h, t), base + h)))

    q_spec, qh_spec = sect(COL_QB)
    k_spec, kh_spec = sect(COL_KB)
    v_spec, vh_spec = sect(COL_VB)
    z_spec, _ = sect(COL_ZB)

    def conv_spec(col0):
        base = col0 // width
        return pl.BlockSpec((CONV_K, width), lambda b, h, t: (0, base + h))

    col_spec = pl.BlockSpec((SLAB, 128), lambda b, h, t: (rows(b, h, t), 0))
    return pl.pallas_call(
        _gdn_kernel,
        grid=(batch, H_B // HEADS_PER_STEP, nt),
        in_specs=[q_spec, k_spec, v_spec, qh_spec, kh_spec, vh_spec,
                  conv_spec(0), conv_spec(H_B * DK), conv_spec(2 * H_B * DK),
                  col_spec, col_spec, col_spec,
                  pl.BlockSpec((H_B, SLAB), lambda b, h, t: (0, rows(b, h, t))),
                  z_spec,
                  pl.BlockSpec((1, DV), lambda b, h, t: (0, 0))],
        out_specs=pl.BlockSpec((SLAB, width), lambda b, h, t: (rows(b, h, t), h)),
        out_shape=jax.ShapeDtypeStruct((batch * seq, WIDTH_B), BF16),
        scratch_shapes=[pltpu.VMEM((HEADS_PER_STEP, DK, DV), F32),
                        pltpu.VMEM((HALO + SLAB, width), F32),
                        pltpu.VMEM((HALO + SLAB, width), F32),
                        pltpu.VMEM((HALO + SLAB, width), F32)],
        compiler_params=pltpu.CompilerParams(
            dimension_semantics=("parallel", "parallel", "arbitrary"),
            vmem_limit_bytes=VMEM_LIMIT),
        name="gated_deltanet",
    )(proj_main, proj_main, proj_main, proj_main, proj_main, proj_main,
      conv_w, conv_w, conv_w, beta, gc, gtail, gcrow, proj_main, norm_g.reshape(1, DV))


def _merge_kernel(oa_ref, ob_ref, wa_ref, wb_ref, ga_ref, gb_ref, o_ref, wab_ref, wbb_ref):
    @pl.when(pl.program_id(1) == 0)
    def _():
        wab_ref[...] = wa_ref[...].astype(BF16)
        wbb_ref[...] = wb_ref[...].astype(BF16)

    ya = jnp.dot(oa_ref[...], wab_ref[...], preferred_element_type=F32)
    yb = jnp.dot(ob_ref[...], wbb_ref[...], preferred_element_type=F32)
    merged = jax.nn.sigmoid(ga_ref[...]) * ya + jax.nn.sigmoid(gb_ref[...]) * yb
    o_ref[...] = merged.astype(o_ref.dtype)


def _merge(o_a, o_b, w_pa, w_pb, layer, gates, tm):
    m = o_a.shape[0]
    gb_blk = D_MODEL // W_TILE
    return pl.pallas_call(
        _merge_kernel,
        grid=(D_MODEL // W_TILE, m // tm),
        in_specs=[pl.BlockSpec((tm, WIDTH_A), lambda j, i: (i, 0)),
                  pl.BlockSpec((tm, WIDTH_B), lambda j, i: (i, 0)),
                  pl.BlockSpec((None, WIDTH_A, W_TILE), lambda j, i: (layer, 0, j)),
                  pl.BlockSpec((None, WIDTH_B, W_TILE), lambda j, i: (layer, 0, j)),
                  pl.BlockSpec((tm, W_TILE), lambda j, i: (i, j)),
                  pl.BlockSpec((tm, W_TILE), lambda j, i: (i, gb_blk + j))],
        out_specs=pl.BlockSpec((tm, W_TILE), lambda j, i: (i, j)),
        out_shape=jax.ShapeDtypeStruct((m, D_MODEL), BF16),
        scratch_shapes=[pltpu.VMEM((WIDTH_A, W_TILE), BF16), pltpu.VMEM((WIDTH_B, W_TILE), BF16)],
        compiler_params=pltpu.CompilerParams(dimension_semantics=("parallel", "arbitrary"),
                                             vmem_limit_bytes=VMEM_LIMIT),
        name="branch_merge",
    )(o_a, o_b, w_pa, w_pb, gates, gates)


def _main_out_tile(j):
    last = COL_KA // W_TILE
    return jnp.where(j == SRC_KV_TILE, last, jnp.where((j > SRC_KV_TILE) & (j <= last), j - 1, j))


def _layer(x2, batch, seq, layer, pre_g, w_in, sinks, conv_w, a_log, dt_bias, gdn_g, w_pa, w_pb, w_o,
           post_g):
    h = _rmsnorm(x2, pre_g, 256)
    proj = _matmul_f32w(h, w_in, layer, N_MAIN // W_TILE, _main_out_tile, 1024, F32, "in_proj_main")
    w_gates = w_in[layer, :, SRC_GATES:].astype(BF16)
    gates = _matmul(h, w_gates, 1024, 1024, F32, "in_proj_gates")
    o_a = _attention(proj, sinks, batch, seq)
    alpha_t = proj[:, COL_BA + ALPHA_LANE:COL_BA + ALPHA_LANE + H_B].T
    beta, gc, gtail, gcrow = _gates(proj, alpha_t, a_log, dt_bias)
    o_b = _gated_deltanet(proj, conv_w, beta, gc, gtail, gcrow, gdn_g, batch, seq)
    merged = _merge(o_a, o_b, w_pa, w_pb, layer, gates, 1024)
    out = _matmul_f32w(merged, w_o, layer, D_MODEL // W_TILE, lambda j: j, 1024, F32, "out_proj")
    return _postnorm_residual(x2, out, post_g, 256)


def kernel(x, pre_norm_g, w_in, sinks, conv_w, a_log, dt_bias, gdn_norm_g, w_pa, w_pb, w_o, post_norm_g):
    batch, seq, d = x.shape
    x2 = x.reshape(batch * seq, d)
    for layer in range(w_in.shape[0]):
        x2 = _layer(x2, batch, seq, layer, pre_norm_g[layer], w_in, sinks[layer], conv_w[layer],
                    a_log[layer], dt_bias[layer], gdn_norm_g[layer], w_pa, w_pb, w_o,
                    post_norm_g[layer])
    return x2.reshape(batch, seq, d)
```

```python
import functools

import jax
import jax.numpy as jnp
from jax import lax
from jax.experimental import pallas as pl
from jax.experimental.pallas import tpu as pltpu

F32 = jnp.float32
BF16 = jnp.bfloat16

D_MODEL = 4096
HD_A = 64
H_A = 32
H_KV = 4
GROUP = H_A // H_KV
WIDTH_A = H_A * HD_A
WINDOW = 128
H_B = 16
DK = 128
DV = 128
WIDTH_B = H_B * DV
CONV_K = 4
CHUNK = 64
EPS = 1e-6

SLAB = 256
HALO = 8
HEADS_PER_STEP = 8
ALPHA_LANE = H_B

VMEM_LIMIT = 56 * 1024 * 1024

COL_QA = 0
COL_ZA = COL_QA + WIDTH_A
COL_QB = COL_ZA + WIDTH_A
COL_KB = COL_QB + H_B * DK
COL_VB = COL_KB + H_B * DK
COL_ZB = COL_VB + H_B * DV
KV_A = H_KV * HD_A
COL_KA = COL_ZB + WIDTH_B
COL_VA = COL_KA + KV_A
COL_BA = COL_VA + KV_A
W_TILE = 512
N_MAIN = COL_BA + W_TILE
SRC_KV_TILE = WIDTH_A // W_TILE
SRC_BA = 2 * WIDTH_A + 2 * KV_A + 2 * H_B * DK + H_B * DV + WIDTH_B
GATE_SHIFT = 2 * H_B
LANE = 128


def _bdot(a, b):
    return jnp.dot(a.astype(BF16), b.astype(BF16), preferred_element_type=F32)


def _bdot_nt(a, b):
    return lax.dot_general(a.astype(BF16), b.astype(BF16), (((1,), (1,)), ((), ())),
                           preferred_element_type=F32)


def _bdot_tn(a, b):
    return lax.dot_general(a.astype(BF16), b.astype(BF16), (((0,), (0,)), ((), ())),
                           preferred_element_type=F32)


def _silu(x):
    return x * jax.nn.sigmoid(x)


def _rmsnorm_kernel(x_ref, g_ref, o_ref):
    x = x_ref[...]
    ms = jnp.mean(x * x, axis=-1, keepdims=True)
    o_ref[...] = (x * lax.rsqrt(ms + EPS) * g_ref[...]).astype(o_ref.dtype)


def _rmsnorm(x, g, rows):
    m, d = x.shape
    return pl.pallas_call(
        _rmsnorm_kernel,
        grid=(m // rows,),
        in_specs=[pl.BlockSpec((rows, d), lambda i: (i, 0)),
                  pl.BlockSpec((1, d), lambda i: (0, 0))],
        out_specs=pl.BlockSpec((rows, d), lambda i: (i, 0)),
        out_shape=jax.ShapeDtypeStruct((m, d), BF16),
        compiler_params=pltpu.CompilerParams(dimension_semantics=("parallel",),
                                             vmem_limit_bytes=VMEM_LIMIT),
        name="pre_norm",
    )(x, g.reshape(1, d))


def _postnorm_kernel(x_ref, y_ref, g_ref, o_ref):
    y = y_ref[...]
    ms = jnp.mean(y * y, axis=-1, keepdims=True)
    o_ref[...] = x_ref[...] + y * lax.rsqrt(ms + EPS) * g_ref[...]


def _postnorm_prenorm_kernel(x_ref, y_ref, g_ref, gn_ref, o_ref, h_ref):
    y = y_ref[...]
    ms = jnp.mean(y * y, axis=-1, keepdims=True)
    x = x_ref[...] + y * lax.rsqrt(ms + EPS) * g_ref[...]
    o_ref[...] = x
    ms = jnp.mean(x * x, axis=-1, keepdims=True)
    h_ref[...] = (x * lax.rsqrt(ms + EPS) * gn_ref[...]).astype(h_ref.dtype)


def _postnorm_residual(x, y, g, next_pre_g, rows):
    m, d = x.shape
    row_spec = pl.BlockSpec((rows, d), lambda i: (i, 0))
    gain_spec = pl.BlockSpec((1, d), lambda i: (0, 0))
    params = pltpu.CompilerParams(dimension_semantics=("parallel",), vmem_limit_bytes=VMEM_LIMIT)
    if next_pre_g is None:
        out = pl.pallas_call(
            _postnorm_kernel, grid=(m // rows,),
            in_specs=[row_spec, row_spec, gain_spec], out_specs=row_spec,
            out_shape=jax.ShapeDtypeStruct((m, d), F32),
            compiler_params=params, name="post_norm",
        )(x, y, g.reshape(1, d))
        return out, None
    return pl.pallas_call(
        _postnorm_prenorm_kernel, grid=(m // rows,),
        in_specs=[row_spec, row_spec, gain_spec, gain_spec], out_specs=[row_spec, row_spec],
        out_shape=[jax.ShapeDtypeStruct((m, d), F32), jax.ShapeDtypeStruct((m, d), BF16)],
        compiler_params=params, name="post_pre_norm",
    )(x, y, g.reshape(1, d), next_pre_g.reshape(1, d))


def _matmul_kernel(a_ref, b_ref, o_ref):
    o_ref[...] = jnp.dot(a_ref[...], b_ref[...], preferred_element_type=F32).astype(o_ref.dtype)


def _matmul(a, b_stack, layer, tm, tn, out_dtype, name):
    m, k = a.shape
    _, _, n = b_stack.shape
    return pl.pallas_call(
        _matmul_kernel,
        grid=(m // tm, n // tn),
        in_specs=[pl.BlockSpec((tm, k), lambda i, j: (i, 0)),
                  pl.BlockSpec((None, k, tn), lambda i, j: (layer, 0, j))],
        out_specs=pl.BlockSpec((tm, tn), lambda i, j: (i, j)),
        out_shape=jax.ShapeDtypeStruct((m, n), out_dtype),
        compiler_params=pltpu.CompilerParams(dimension_semantics=("parallel", "parallel"),
                                             vmem_limit_bytes=VMEM_LIMIT),
        name=name,
    )(a, b_stack)


def _cast_tiles_kernel(w_ref, o_ref):
    o_ref[...] = w_ref[...].astype(o_ref.dtype)


def _cast_tiles(w_stack, n_tiles, src_tile, name):
    depth, k, _ = w_stack.shape
    return pl.pallas_call(
        _cast_tiles_kernel,
        grid=(depth, n_tiles),
        in_specs=[pl.BlockSpec((None, k, W_TILE), lambda l, j: (l, 0, src_tile(j)))],
        out_specs=pl.BlockSpec((None, k, W_TILE), lambda l, j: (l, 0, j)),
        out_shape=jax.ShapeDtypeStruct((depth, k, n_tiles * W_TILE), BF16),
        compiler_params=pltpu.CompilerParams(dimension_semantics=("parallel", "parallel"),
                                             vmem_limit_bytes=VMEM_LIMIT),
        name=name,
    )(w_stack)


def _cast_shifted_kernel(a_ref, b_ref, o_ref):
    both = jnp.concatenate([a_ref[...], b_ref[...]], axis=1)
    o_ref[...] = both[:, GATE_SHIFT:GATE_SHIFT + W_TILE].astype(o_ref.dtype)


def _cast_gate_tiles(w_stack):
    depth, k, _ = w_stack.shape
    first = SRC_BA // W_TILE
    n_tiles = 2 * D_MODEL // W_TILE
    lanes_per_tile = W_TILE // LANE
    return pl.pallas_call(
        _cast_shifted_kernel,
        grid=(depth, n_tiles),
        in_specs=[pl.BlockSpec((None, k, W_TILE), lambda l, j: (l, 0, first + j)),
                  pl.BlockSpec((None, k, LANE), lambda l, j: (l, 0, (first + j + 1) * lanes_per_tile))],
        out_specs=pl.BlockSpec((None, k, W_TILE), lambda l, j: (l, 0, j)),
        out_shape=jax.ShapeDtypeStruct((depth, k, n_tiles * W_TILE), BF16),
        compiler_params=pltpu.CompilerParams(dimension_semantics=("parallel", "parallel"),
                                             vmem_limit_bytes=VMEM_LIMIT),
        name="cast_gate_weights",
    )(w_stack, w_stack)


def _attn_kernel(sinks_ref, q_ref, z_ref, kc_ref, kp_ref, vc_ref, vp_ref, o_ref):
    n = pl.program_id(1)
    blk = q_ref.shape[0]
    qi = lax.broadcasted_iota(jnp.int32, (blk, 2 * blk), 0)
    kj = lax.broadcasted_iota(jnp.int32, (blk, 2 * blk), 1)
    dist = qi + blk - kj
    valid = (dist >= 0) & (dist < WINDOW) & ((kj >= blk) | (n > 0))
    distf = dist.astype(F32)
    kband = jnp.concatenate([kp_ref[...], kc_ref[...]], axis=0).astype(BF16)
    vband = jnp.concatenate([vp_ref[...], vc_ref[...]], axis=0).astype(BF16)
    scale = HD_A ** -0.5
    for h in range(H_KV):
        kh = kband[:, h * HD_A:(h + 1) * HD_A]
        vh = vband[:, h * HD_A:(h + 1) * HD_A]
        heads = [h * GROUP + g for g in range(GROUP)]
        cols = [slice(hh * HD_A, (hh + 1) * HD_A) for hh in heads]
        s = [_bdot_nt(q_ref[:, c], kh) for c in cols]
        p, denom = [], []
        for hh, s_h in zip(heads, s):
            slope = 2.0 ** (-8.0 * (hh + 1) / H_A)
            s_h = jnp.where(valid, s_h * scale - slope * distf, -1e30)
            sink = sinks_ref[hh]
            m = jnp.maximum(jnp.max(s_h, axis=-1, keepdims=True), sink)
            p_h = jnp.exp(s_h - m)
            denom.append(jnp.sum(p_h, axis=-1, keepdims=True) + jnp.exp(sink - m))
            p.append(p_h.astype(BF16))
        o = _bdot(jnp.concatenate(p, axis=0), vh)
        for g, c in enumerate(cols):
            o_h = o[g * blk:(g + 1) * blk] / denom[g]
            o_ref[:, c] = (o_h * _silu(z_ref[:, c])).astype(o_ref.dtype)


def _attention(proj, sinks, batch, seq):
    nb = seq // WINDOW
    k_blk = COL_KA // KV_A
    v_blk = COL_VA // KV_A

    def row(b, n):
        return b * nb + n

    def prev_row(b, n):
        return b * nb + jnp.maximum(n - 1, 0)

    return pl.pallas_call(
        _attn_kernel,
        grid=(batch, nb),
        in_specs=[
            pl.BlockSpec(memory_space=pltpu.SMEM),
            pl.BlockSpec((WINDOW, WIDTH_A), lambda b, n: (row(b, n), COL_QA // WIDTH_A)),
            pl.BlockSpec((WINDOW, WIDTH_A), lambda b, n: (row(b, n), COL_ZA // WIDTH_A)),
            pl.BlockSpec((WINDOW, KV_A), lambda b, n: (row(b, n), k_blk)),
            pl.BlockSpec((WINDOW, KV_A), lambda b, n: (prev_row(b, n), k_blk)),
            pl.BlockSpec((WINDOW, KV_A), lambda b, n: (row(b, n), v_blk)),
            pl.BlockSpec((WINDOW, KV_A), lambda b, n: (prev_row(b, n), v_blk)),
        ],
        out_specs=pl.BlockSpec((WINDOW, WIDTH_A), lambda b, n: (row(b, n), 0)),
        out_shape=jax.ShapeDtypeStruct((batch * seq, WIDTH_A), BF16),
        compiler_params=pltpu.CompilerParams(dimension_semantics=("parallel", "arbitrary"),
                                             vmem_limit_bytes=VMEM_LIMIT),
        name="swa_attention",
    )(sinks, proj, proj, proj, proj, proj, proj)


def _softplus(x):
    return jnp.maximum(x, 0.0) + jnp.log1p(jnp.exp(-jnp.abs(x)))


def _chunk_scans(g, axis):
    size = g.shape[axis]
    pos = lax.broadcasted_iota(jnp.int32, g.shape, axis) % CHUNK
    pre = g
    suf = g
    step = 1
    while step < CHUNK:
        pre = pre + jnp.where(pos >= step, pltpu.roll(pre, step, axis=axis), 0.0)
        suf = suf + jnp.where(pos < CHUNK - step, pltpu.roll(suf, size - step, axis=axis), 0.0)
        step *= 2
    return pre, suf - g


def _gate_kernel(ba_ref, alt_ref, alog_row_ref, dtb_row_ref, alog_col_ref, dtb_col_ref,
                 beta_ref, gc_ref, gtail_ref, gcrow_ref):
    ba = ba_ref[...]
    beta_ref[...] = jax.nn.sigmoid(ba)
    g = -jnp.exp(alog_row_ref[...]) * _softplus(ba + dtb_row_ref[...])
    gc, gtail = _chunk_scans(g, 0)
    gc_ref[...] = gc
    gtail_ref[...] = gtail
    g_row = -jnp.exp(alog_col_ref[...]) * _softplus(alt_ref[...] + dtb_col_ref[...])
    gcrow_ref[...] = _chunk_scans(g_row, 1)[0]


def _gates(proj, alpha_t, a_log, dt_bias):
    rows = proj.shape[0]
    lane_pad = (ALPHA_LANE, 128 - ALPHA_LANE - H_B)
    alog_row = jnp.pad(a_log, lane_pad).reshape(1, 128)
    dtb_row = jnp.pad(dt_bias, lane_pad).reshape(1, 128)
    ba_blk = COL_BA // 128
    col_out = jax.ShapeDtypeStruct((rows, 128), F32)
    col_spec = pl.BlockSpec((SLAB, 128), lambda i: (i, 0))
    small = lambda shape: pl.BlockSpec(shape, lambda i: (0, 0))
    return pl.pallas_call(
        _gate_kernel,
        grid=(rows // SLAB,),
        in_specs=[pl.BlockSpec((SLAB, 128), lambda i: (i, ba_blk)),
                  pl.BlockSpec((H_B, SLAB), lambda i: (0, i)),
                  small((1, 128)), small((1, 128)), small((H_B, 1)), small((H_B, 1))],
        out_specs=[col_spec, col_spec, col_spec, pl.BlockSpec((H_B, SLAB), lambda i: (0, i))],
        out_shape=[col_out, col_out, col_out, jax.ShapeDtypeStruct((H_B, rows), F32)],
        compiler_params=pltpu.CompilerParams(dimension_semantics=("parallel",),
                                             vmem_limit_bytes=VMEM_LIMIT),
        name="gdn_gates",
    )(proj, alpha_t, alog_row, dtb_row, a_log.reshape(H_B, 1), dt_bias.reshape(H_B, 1))


def _unit_lower_inverse(a_list, same16, same32):
    n = a_list[0].shape[0]
    eye = (lax.broadcasted_iota(jnp.int32, (n, n), 0) ==
           lax.broadcasted_iota(jnp.int32, (n, n), 1)).astype(F32)
    d = [jnp.where(same16, a, 0.0) for a in a_list]
    db = [x.astype(BF16) for x in d]
    d2 = [_bdot(x, x).astype(BF16) for x in db]
    t = [eye - x for x in d]
    d4 = [_bdot(x, x).astype(BF16) for x in d2]
    t = [x + _bdot(x, y) for x, y in zip(t, d2)]
    d8 = [_bdot(x, x) for x in d4]
    t = [x + _bdot(x, y) for x, y in zip(t, d4)]
    t = [x + _bdot(x, y) for x, y in zip(t, d8)]
    for lower, upper in ((same16, same32), (same32, None)):
        off = ~lower if upper is None else upper & ~lower
        e = [jnp.where(off, a, 0.0) for a in a_list]
        tb = [x.astype(BF16) for x in t]
        te = [_bdot(x, y) for x, y in zip(tb, e)]
        t = [x - _bdot(y, z) for x, y, z in zip(t, te, tb)]
    return t


def _gdn_kernel(q_ref, k_ref, v_ref, qh_ref, kh_ref, vh_ref, wq_ref, wk_ref, wv_ref,
                beta_ref, gc_ref, gtail_ref, gcrow_ref, z_ref, ng_ref, o_ref,
                state_ref, qx_ref, kx_ref, vx_ref):
    hg = pl.program_id(1)
    t = pl.program_id(2)
    n = q_ref.shape[0]

    @pl.when(t == 0)
    def _():
        state_ref[...] = jnp.zeros_like(state_ref)

    def conv_silu(x_ref, halo_ref, w_ref, xx_ref):
        xx = jnp.concatenate([jnp.where(t > 0, halo_ref[...], 0.0), x_ref[...]], axis=0)
        acc = w_ref[CONV_K - 1:CONV_K, :] * xx[HALO:]
        for back in range(1, CONV_K):
            tap = CONV_K - 1 - back
            acc = acc + w_ref[tap:tap + 1, :] * pltpu.roll(xx, back, axis=0)[HALO:]
        return _silu(acc)

    q_all = conv_silu(q_ref, qh_ref, wq_ref, qx_ref)
    k_all = conv_silu(k_ref, kh_ref, wk_ref, kx_ref)
    v_all = conv_silu(v_ref, vh_ref, wv_ref, vx_ref)

    lane = lax.broadcasted_iota(jnp.int32, (n, 128), 1)

    def pick_lane(ref, idx):
        return jnp.sum(jnp.where(lane == idx, ref[...], 0.0), axis=-1, keepdims=True)

    row = lax.broadcasted_iota(jnp.int32, (n, n), 0)
    col = lax.broadcasted_iota(jnp.int32, (n, n), 1)
    same_chunk = (row // CHUNK) == (col // CHUNK)
    causal = same_chunk & (row >= col)
    strict = same_chunk & (row > col)
    same16 = (row // 16) == (col // 16)
    same32 = (row // 32) == (col // 32)

    heads = range(HEADS_PER_STEP)
    lanes = [slice(hl * DK, (hl + 1) * DK) for hl in heads]
    head = [hg * HEADS_PER_STEP + hl for hl in heads]
    q = [q_all[:, s] for s in lanes]
    k = [k_all[:, s] for s in lanes]
    v = [v_all[:, s] for s in lanes]
    q = [x * (lax.rsqrt(jnp.sum(x * x, axis=-1, keepdims=True) + EPS) * (DK ** -0.5)) for x in q]
    k = [x * lax.rsqrt(jnp.sum(x * x, axis=-1, keepdims=True) + EPS) for x in k]
    beta = [pick_lane(beta_ref, h) for h in head]
    gc = [pick_lane(gc_ref, ALPHA_LANE + h) for h in head]
    gtail = [pick_lane(gtail_ref, ALPHA_LANE + h) for h in head]
    gcrow = [gcrow_ref[pl.ds(h, 1), :] for h in head]
    kb = [x * b for x, b in zip(k, beta)]
    gram = [_bdot_nt(jnp.concatenate([x, y], axis=0), z) for x, y, z in zip(kb, q, k)]
    decay = [jnp.exp(jnp.where(causal, c - r, -1e30)) for c, r in zip(gc, gcrow)]
    a = [jnp.where(strict, g[:n] * d, 0.0) for g, d in zip(gram, decay)]
    attn = [(g[n:] * d).astype(BF16) for g, d in zip(gram, decay)]
    tinv = _unit_lower_inverse(a, same16, same32)
    eg = [jnp.exp(c) for c in gc]
    rhs = [jnp.concatenate([x * b, y * e], axis=1) for x, b, y, e in zip(v, beta, kb, eg)]
    sol = [_bdot(x, y) for x, y in zip(tinv, rhs)]
    u = [x[:, :DV] for x in sol]
    w = [x[:, DV:].astype(BF16) for x in sol]
    qd = [(x * e).astype(BF16) for x, e in zip(q, eg)]
    kt = [(x * jnp.exp(g)).astype(BF16) for x, g in zip(k, gtail)]
    state = [state_ref[hl] for hl in heads]
    v_new = [[] for _ in heads]
    o_state = [[] for _ in heads]
    for c in range(n // CHUNK):
        rows = slice(c * CHUNK, (c + 1) * CHUNK)
        last = (c + 1) * CHUNK - 1
        lhs = [jnp.concatenate([x[rows], y[rows]], axis=0) for x, y in zip(w, qd)]
        ws = [_bdot(x, s) for x, s in zip(lhs, state)]
        vn = [x[rows] - y[:CHUNK] for x, y in zip(u, ws)]
        for hl in heads:
            v_new[hl].append(vn[hl])
            o_state[hl].append(ws[hl][CHUNK:])
        chunk_decay = [jnp.exp(g[last:last + 1, :]) for g in gc]
        state = [s * d + _bdot_tn(x[rows], y) for s, d, x, y in zip(state, chunk_decay, kt, vn)]
    for hl in heads:
        state_ref[hl] = state[hl]
    o = [jnp.concatenate(x, axis=0) + _bdot(y, jnp.concatenate(z, axis=0))
         for x, y, z in zip(o_state, attn, v_new)]
    for hl in heads:
        y = o[hl]
        y = y * lax.rsqrt(jnp.mean(y * y, axis=-1, keepdims=True) + EPS) * ng_ref[...]
        o_ref[:, lanes[hl]] = (y * _silu(z_ref[:, lanes[hl]])).astype(o_ref.dtype)


def _gated_deltanet(proj_main, conv_w, beta, gc, gtail, gcrow, norm_g, batch, seq):
    width = HEADS_PER_STEP * DK
    nt = seq // SLAB
    halo_per_slab = SLAB // HALO

    def rows(b, h, t):
        return b * nt + t

    def halo_rows(b, h, t):
        return jnp.maximum((b * nt + t) * halo_per_slab - 1, 0)

    def sect(col0):
        base = col0 // width
        return (pl.BlockSpec((SLAB, width), lambda b, h, t: (rows(b, h, t), base + h)),
                pl.BlockSpec((HALO, width), lambda b, h, t: (halo_rows(b, h, t), base + h)))

    q_spec, qh_spec = sect(COL_QB)
    k_spec, kh_spec = sect(COL_KB)
    v_spec, vh_spec = sect(COL_VB)
    z_spec, _ = sect(COL_ZB)

    def conv_spec(col0):
        base = col0 // width
        return pl.BlockSpec((CONV_K, width), lambda b, h, t: (0, base + h))

    col_spec = pl.BlockSpec((SLAB, 128), lambda b, h, t: (rows(b, h, t), 0))
    return pl.pallas_call(
        _gdn_kernel,
        grid=(batch, H_B // HEADS_PER_STEP, nt),
        in_specs=[q_spec, k_spec, v_spec, qh_spec, kh_spec, vh_spec,
                  conv_spec(0), conv_spec(H_B * DK), conv_spec(2 * H_B * DK),
                  col_spec, col_spec, col_spec,
                  pl.BlockSpec((H_B, SLAB), lambda b, h, t: (0, rows(b, h, t))),
                  z_spec,
                  pl.BlockSpec((1, DV), lambda b, h, t: (0, 0))],
        out_specs=pl.BlockSpec((SLAB, width), lambda b, h, t: (rows(b, h, t), h)),
        out_shape=jax.ShapeDtypeStruct((batch * seq, WIDTH_B), BF16),
        scratch_shapes=[pltpu.VMEM((HEADS_PER_STEP, DK, DV), F32),
                        pltpu.VMEM((HALO + SLAB, width), F32),
                        pltpu.VMEM((HALO + SLAB, width), F32),
                        pltpu.VMEM((HALO + SLAB, width), F32)],
        compiler_params=pltpu.CompilerParams(
            dimension_semantics=("parallel", "parallel", "arbitrary"),
            vmem_limit_bytes=VMEM_LIMIT),
        name="gated_deltanet",
    )(proj_main, proj_main, proj_main, proj_main, proj_main, proj_main,
      conv_w, conv_w, conv_w, beta, gc, gtail, gcrow, proj_main, norm_g.reshape(1, DV))


def _merge_kernel(oa_ref, ob_ref, wa_ref, wb_ref, ga_ref, gb_ref, o_ref):
    ya = jnp.dot(oa_ref[...], wa_ref[...], preferred_element_type=F32)
    yb = jnp.dot(ob_ref[...], wb_ref[...], preferred_element_type=F32)
    merged = jax.nn.sigmoid(ga_ref[...]) * ya + jax.nn.sigmoid(gb_ref[...]) * yb
    o_ref[...] = merged.astype(o_ref.dtype)


def _merge(o_a, o_b, w_pa, w_pb, layer, gates, tm, tn):
    m = o_a.shape[0]
    gb_blk = D_MODEL // tn
    return pl.pallas_call(
        _merge_kernel,
        grid=(m // tm, D_MODEL // tn),
        in_specs=[pl.BlockSpec((tm, WIDTH_A), lambda i, j: (i, 0)),
                  pl.BlockSpec((tm, WIDTH_B), lambda i, j: (i, 0)),
                  pl.BlockSpec((None, WIDTH_A, tn), lambda i, j: (layer, 0, j)),
                  pl.BlockSpec((None, WIDTH_B, tn), lambda i, j: (layer, 0, j)),
                  pl.BlockSpec((tm, tn), lambda i, j: (i, j)),
                  pl.BlockSpec((tm, tn), lambda i, j: (i, gb_blk + j))],
        out_specs=pl.BlockSpec((tm, tn), lambda i, j: (i, j)),
        out_shape=jax.ShapeDtypeStruct((m, D_MODEL), BF16),
        compiler_params=pltpu.CompilerParams(dimension_semantics=("parallel", "parallel"),
                                             vmem_limit_bytes=VMEM_LIMIT),
        name="branch_merge",
    )(o_a, o_b, w_pa, w_pb, gates, gates)


def _main_src_tile(j):
    last = COL_KA // W_TILE
    return jnp.where(j < SRC_KV_TILE, j, jnp.where(j < last, j + 1, jnp.where(j == last, SRC_KV_TILE, j)))


def _layer(x2, h, batch, seq, layer, wb_main, wb_gates, sinks, conv_w, a_log, dt_bias, gdn_g,
           wb_pa, wb_pb, wb_o, post_g, next_pre_g):
    proj = _matmul(h, wb_main, layer, 1024, 1024, F32, "in_proj_main")
    gates = _matmul(h, wb_gates, layer, 1024, 1024, F32, "in_proj_gates")
    o_a = _attention(proj, sinks, batch, seq)
    alpha_t = proj[:, COL_BA + ALPHA_LANE:COL_BA + ALPHA_LANE + H_B].T
    beta, gc, gtail, gcrow = _gates(proj, alpha_t, a_log, dt_bias)
    o_b = _gated_deltanet(proj, conv_w, beta, gc, gtail, gcrow, gdn_g, batch, seq)
    merged = _merge(o_a, o_b, wb_pa, wb_pb, layer, gates, 1024, 512)
    out = _matmul(merged, wb_o, layer, 1024, 1024, F32, "out_proj")
    return _postnorm_residual(x2, out, post_g, next_pre_g, 256)


def kernel(x, pre_norm_g, w_in, sinks, conv_w, a_log, dt_bias, gdn_norm_g, w_pa, w_pb, w_o, post_norm_g):
    batch, seq, d = x.shape
    depth = w_in.shape[0]
    x2 = x.reshape(batch * seq, d)
    wb_main = _cast_tiles(w_in, N_MAIN // W_TILE, _main_src_tile, "cast_in_weights")
    wb_gates = _cast_gate_tiles(w_in)
    identity = lambda j: j
    wb_pa = _cast_tiles(w_pa, D_MODEL // W_TILE, identity, "cast_pa_weights")
    wb_pb = _cast_tiles(w_pb, D_MODEL // W_TILE, identity, "cast_pb_weights")
    wb_o = _cast_tiles(w_o, D_MODEL // W_TILE, identity, "cast_out_weights")
    h = _rmsnorm(x2, pre_norm_g[0], 256)
    for layer in range(depth):
        next_pre_g = pre_norm_g[layer + 1] if layer + 1 < depth else None
        x2, h = _layer(x2, h, batch, seq, layer, wb_main, wb_gates, sinks[layer], conv_w[layer],
                       a_log[layer], dt_bias[layer], gdn_norm_g[layer], wb_pa, wb_pb, wb_o,
                       post_norm_g[layer], next_pre_g)
    return x2.reshape(batch, seq, d)
```

```python
import functools

import jax
import jax.numpy as jnp
from jax import lax
from jax.experimental import pallas as pl
from jax.experimental.pallas import tpu as pltpu

F32 = jnp.float32
BF16 = jnp.bfloat16

D_MODEL = 4096
HD_A = 64
H_A = 32
H_KV = 4
GROUP = H_A // H_KV
WIDTH_A = H_A * HD_A
WINDOW = 128
H_B = 16
DK = 128
DV = 128
WIDTH_B = H_B * DV
CONV_K = 4
CHUNK = 64
EPS = 1e-6

SLAB = 256
HALO = 8
HEADS_PER_STEP = 8
ALPHA_LANE = H_B

VMEM_LIMIT = 56 * 1024 * 1024

COL_QA = 0
COL_ZA = COL_QA + WIDTH_A
COL_QB = COL_ZA + WIDTH_A
COL_KB = COL_QB + H_B * DK
COL_VB = COL_KB + H_B * DK
COL_ZB = COL_VB + H_B * DV
KV_A = H_KV * HD_A
COL_KA = COL_ZB + WIDTH_B
COL_VA = COL_KA + KV_A
COL_BA = COL_VA + KV_A
W_TILE = 512
N_MAIN = COL_BA + W_TILE
SRC_KV_TILE = WIDTH_A // W_TILE
SRC_BA = 2 * WIDTH_A + 2 * KV_A + 2 * H_B * DK + H_B * DV + WIDTH_B
GATE_SHIFT = 2 * H_B
LANE = 128


def _bdot(a, b):
    return jnp.dot(a.astype(BF16), b.astype(BF16), preferred_element_type=F32)


def _bdot_nt(a, b):
    return lax.dot_general(a.astype(BF16), b.astype(BF16), (((1,), (1,)), ((), ())),
                           preferred_element_type=F32)


def _bdot_tn(a, b):
    return lax.dot_general(a.astype(BF16), b.astype(BF16), (((0,), (0,)), ((), ())),
                           preferred_element_type=F32)


def _silu(x):
    return x * jax.nn.sigmoid(x)


def _rmsnorm_kernel(x_ref, g_ref, o_ref):
    x = x_ref[...]
    ms = jnp.mean(x * x, axis=-1, keepdims=True)
    o_ref[...] = (x * lax.rsqrt(ms + EPS) * g_ref[...]).astype(o_ref.dtype)


def _rmsnorm(x, g, rows):
    m, d = x.shape
    return pl.pallas_call(
        _rmsnorm_kernel,
        grid=(m // rows,),
        in_specs=[pl.BlockSpec((rows, d), lambda i: (i, 0)),
                  pl.BlockSpec((1, d), lambda i: (0, 0))],
        out_specs=pl.BlockSpec((rows, d), lambda i: (i, 0)),
        out_shape=jax.ShapeDtypeStruct((m, d), BF16),
        compiler_params=pltpu.CompilerParams(dimension_semantics=("parallel",),
                                             vmem_limit_bytes=VMEM_LIMIT),
        name="pre_norm",
    )(x, g.reshape(1, d))


def _postnorm_kernel(x_ref, y_ref, g_ref, o_ref):
    y = y_ref[...]
    ms = jnp.mean(y * y, axis=-1, keepdims=True)
    o_ref[...] = x_ref[...] + y * lax.rsqrt(ms + EPS) * g_ref[...]


def _postnorm_prenorm_kernel(x_ref, y_ref, g_ref, gn_ref, o_ref, h_ref):
    y = y_ref[...]
    ms = jnp.mean(y * y, axis=-1, keepdims=True)
    x = x_ref[...] + y * lax.rsqrt(ms + EPS) * g_ref[...]
    o_ref[...] = x
    ms = jnp.mean(x * x, axis=-1, keepdims=True)
    h_ref[...] = (x * lax.rsqrt(ms + EPS) * gn_ref[...]).astype(h_ref.dtype)


def _postnorm_residual(x, y, g, next_pre_g, rows):
    m, d = x.shape
    row_spec = pl.BlockSpec((rows, d), lambda i: (i, 0))
    gain_spec = pl.BlockSpec((1, d), lambda i: (0, 0))
    params = pltpu.CompilerParams(dimension_semantics=("parallel",), vmem_limit_bytes=VMEM_LIMIT)
    if next_pre_g is None:
        out = pl.pallas_call(
            _postnorm_kernel, grid=(m // rows,),
            in_specs=[row_spec, row_spec, gain_spec], out_specs=row_spec,
            out_shape=jax.ShapeDtypeStruct((m, d), F32),
            compiler_params=params, name="post_norm",
        )(x, y, g.reshape(1, d))
        return out, None
    return pl.pallas_call(
        _postnorm_prenorm_kernel, grid=(m // rows,),
        in_specs=[row_spec, row_spec, gain_spec, gain_spec], out_specs=[row_spec, row_spec],
        out_shape=[jax.ShapeDtypeStruct((m, d), F32), jax.ShapeDtypeStruct((m, d), BF16)],
        compiler_params=params, name="post_pre_norm",
    )(x, y, g.reshape(1, d), next_pre_g.reshape(1, d))


def _matmul_kernel(a_ref, b_ref, o_ref):
    o_ref[...] = jnp.dot(a_ref[...], b_ref[...], preferred_element_type=F32).astype(o_ref.dtype)


def _matmul(a, b_stack, layer, tm, tn, out_dtype, name):
    m, k = a.shape
    _, _, n = b_stack.shape
    return pl.pallas_call(
        _matmul_kernel,
        grid=(m // tm, n // tn),
        in_specs=[pl.BlockSpec((tm, k), lambda i, j: (i, 0)),
                  pl.BlockSpec((None, k, tn), lambda i, j: (layer, 0, j))],
        out_specs=pl.BlockSpec((tm, tn), lambda i, j: (i, j)),
        out_shape=jax.ShapeDtypeStruct((m, n), out_dtype),
        compiler_params=pltpu.CompilerParams(dimension_semantics=("parallel", "parallel"),
                                             vmem_limit_bytes=VMEM_LIMIT),
        name=name,
    )(a, b_stack)


def _cast_tiles_kernel(w_ref, o_ref):
    o_ref[...] = w_ref[...].astype(o_ref.dtype)


def _cast_tiles(w_stack, n_tiles, src_tile, name):
    depth, k, n_src = w_stack.shape
    w_stack = w_stack.reshape(depth * k, n_src)
    return pl.pallas_call(
        _cast_tiles_kernel,
        grid=(depth, n_tiles),
        in_specs=[pl.BlockSpec((k, W_TILE), lambda l, j: (l, src_tile(j)))],
        out_specs=pl.BlockSpec((None, k, W_TILE), lambda l, j: (l, 0, j)),
        out_shape=jax.ShapeDtypeStruct((depth, k, n_tiles * W_TILE), BF16),
        compiler_params=pltpu.CompilerParams(dimension_semantics=("parallel", "parallel"),
                                             vmem_limit_bytes=VMEM_LIMIT),
        name=name,
    )(w_stack)


def _cast_shifted_kernel(a_ref, b_ref, o_ref):
    both = jnp.concatenate([a_ref[...], b_ref[...]], axis=1)
    o_ref[...] = both[:, GATE_SHIFT:GATE_SHIFT + W_TILE].astype(o_ref.dtype)


def _cast_gate_tiles(w_stack):
    depth, k, n_src = w_stack.shape
    w_stack = w_stack.reshape(depth * k, n_src)
    first = SRC_BA // W_TILE
    n_tiles = 2 * D_MODEL // W_TILE
    lanes_per_tile = W_TILE // LANE
    return pl.pallas_call(
        _cast_shifted_kernel,
        grid=(depth, n_tiles),
        in_specs=[pl.BlockSpec((k, W_TILE), lambda l, j: (l, first + j)),
                  pl.BlockSpec((k, LANE), lambda l, j: (l, (first + j + 1) * lanes_per_tile))],
        out_specs=pl.BlockSpec((None, k, W_TILE), lambda l, j: (l, 0, j)),
        out_shape=jax.ShapeDtypeStruct((depth, k, n_tiles * W_TILE), BF16),
        compiler_params=pltpu.CompilerParams(dimension_semantics=("parallel", "parallel"),
                                             vmem_limit_bytes=VMEM_LIMIT),
        name="cast_gate_weights",
    )(w_stack, w_stack)


def _attn_kernel(sinks_ref, q_ref, z_ref, kc_ref, kp_ref, vc_ref, vp_ref, o_ref):
    n = pl.program_id(1)
    blk = q_ref.shape[0]
    qi = lax.broadcasted_iota(jnp.int32, (blk, 2 * blk), 0)
    kj = lax.broadcasted_iota(jnp.int32, (blk, 2 * blk), 1)
    dist = qi + blk - kj
    valid = (dist >= 0) & (dist < WINDOW) & ((kj >= blk) | (n > 0))
    distf = dist.astype(F32)
    kband = jnp.concatenate([kp_ref[...], kc_ref[...]], axis=0).astype(BF16)
    vband = jnp.concatenate([vp_ref[...], vc_ref[...]], axis=0).astype(BF16)
    scale = HD_A ** -0.5
    for h in range(H_KV):
        kh = kband[:, h * HD_A:(h + 1) * HD_A]
        vh = vband[:, h * HD_A:(h + 1) * HD_A]
        heads = [h * GROUP + g for g in range(GROUP)]
        cols = [slice(hh * HD_A, (hh + 1) * HD_A) for hh in heads]
        s = [_bdot_nt(q_ref[:, c], kh) for c in cols]
        p, denom = [], []
        for hh, s_h in zip(heads, s):
            slope = 2.0 ** (-8.0 * (hh + 1) / H_A)
            s_h = jnp.where(valid, s_h * scale - slope * distf, -1e30)
            sink = sinks_ref[hh]
            m = jnp.maximum(jnp.max(s_h, axis=-1, keepdims=True), sink)
            p_h = jnp.exp(s_h - m)
            denom.append(jnp.sum(p_h, axis=-1, keepdims=True) + jnp.exp(sink - m))
            p.append(p_h.astype(BF16))
        o = _bdot(jnp.concatenate(p, axis=0), vh)
        for g, c in enumerate(cols):
            o_h = o[g * blk:(g + 1) * blk] / denom[g]
            o_ref[:, c] = (o_h * _silu(z_ref[:, c])).astype(o_ref.dtype)


def _attention(proj, sinks, batch, seq):
    nb = seq // WINDOW
    k_blk = COL_KA // KV_A
    v_blk = COL_VA // KV_A

    def row(b, n):
        return b * nb + n

    def prev_row(b, n):
        return b * nb + jnp.maximum(n - 1, 0)

    return pl.pallas_call(
        _attn_kernel,
        grid=(batch, nb),
        in_specs=[
            pl.BlockSpec(memory_space=pltpu.SMEM),
            pl.BlockSpec((WINDOW, WIDTH_A), lambda b, n: (row(b, n), COL_QA // WIDTH_A)),
            pl.BlockSpec((WINDOW, WIDTH_A), lambda b, n: (row(b, n), COL_ZA // WIDTH_A)),
            pl.BlockSpec((WINDOW, KV_A), lambda b, n: (row(b, n), k_blk)),
            pl.BlockSpec((WINDOW, KV_A), lambda b, n: (prev_row(b, n), k_blk)),
            pl.BlockSpec((WINDOW, KV_A), lambda b, n: (row(b, n), v_blk)),
            pl.BlockSpec((WINDOW, KV_A), lambda b, n: (prev_row(b, n), v_blk)),
        ],
        out_specs=pl.BlockSpec((WINDOW, WIDTH_A), lambda b, n: (row(b, n), 0)),
        out_shape=jax.ShapeDtypeStruct((batch * seq, WIDTH_A), BF16),
        compiler_params=pltpu.CompilerParams(dimension_semantics=("parallel", "arbitrary"),
                                             vmem_limit_bytes=VMEM_LIMIT),
        name="swa_attention",
    )(sinks, proj, proj, proj, proj, proj, proj)


def _softplus(x):
    return jnp.maximum(x, 0.0) + jnp.log1p(jnp.exp(-jnp.abs(x)))


def _chunk_scans(g, axis):
    size = g.shape[axis]
    pos = lax.broadcasted_iota(jnp.int32, g.shape, axis) % CHUNK
    pre = g
    suf = g
    step = 1
    while step < CHUNK:
        pre = pre + jnp.where(pos >= step, pltpu.roll(pre, step, axis=axis), 0.0)
        suf = suf + jnp.where(pos < CHUNK - step, pltpu.roll(suf, size - step, axis=axis), 0.0)
        step *= 2
    return pre, suf - g


def _gate_kernel(ba_ref, alt_ref, alog_row_ref, dtb_row_ref, alog_col_ref, dtb_col_ref,
                 beta_ref, gc_ref, gtail_ref, gcrow_ref):
    ba = ba_ref[...]
    beta_ref[...] = jax.nn.sigmoid(ba)
    g = -jnp.exp(alog_row_ref[...]) * _softplus(ba + dtb_row_ref[...])
    gc, gtail = _chunk_scans(g, 0)
    gc_ref[...] = gc
    gtail_ref[...] = gtail
    g_row = -jnp.exp(alog_col_ref[...]) * _softplus(alt_ref[...] + dtb_col_ref[...])
    gcrow_ref[...] = _chunk_scans(g_row, 1)[0]


def _gates(proj, alpha_t, a_log, dt_bias):
    rows = proj.shape[0]
    lane_pad = (ALPHA_LANE, 128 - ALPHA_LANE - H_B)
    alog_row = jnp.pad(a_log, lane_pad).reshape(1, 128)
    dtb_row = jnp.pad(dt_bias, lane_pad).reshape(1, 128)
    ba_blk = COL_BA // 128
    col_out = jax.ShapeDtypeStruct((rows, 128), F32)
    col_spec = pl.BlockSpec((SLAB, 128), lambda i: (i, 0))
    small = lambda shape: pl.BlockSpec(shape, lambda i: (0, 0))
    return pl.pallas_call(
        _gate_kernel,
        grid=(rows // SLAB,),
        in_specs=[pl.BlockSpec((SLAB, 128), lambda i: (i, ba_blk)),
                  pl.BlockSpec((H_B, SLAB), lambda i: (0, i)),
                  small((1, 128)), small((1, 128)), small((H_B, 1)), small((H_B, 1))],
        out_specs=[col_spec, col_spec, col_spec, pl.BlockSpec((H_B, SLAB), lambda i: (0, i))],
        out_shape=[col_out, col_out, col_out, jax.ShapeDtypeStruct((H_B, rows), F32)],
        compiler_params=pltpu.CompilerParams(dimension_semantics=("parallel",),
                                             vmem_limit_bytes=VMEM_LIMIT),
        name="gdn_gates",
    )(proj, alpha_t, alog_row, dtb_row, a_log.reshape(H_B, 1), dt_bias.reshape(H_B, 1))


def _unit_lower_inverse(a_list, same16, same32):
    n = a_list[0].shape[0]
    eye = (lax.broadcasted_iota(jnp.int32, (n, n), 0) ==
           lax.broadcasted_iota(jnp.int32, (n, n), 1)).astype(F32)
    d = [jnp.where(same16, a, 0.0) for a in a_list]
    db = [x.astype(BF16) for x in d]
    d2 = [_bdot(x, x).astype(BF16) for x in db]
    t = [eye - x for x in d]
    d4 = [_bdot(x, x).astype(BF16) for x in d2]
    t = [x + _bdot(x, y) for x, y in zip(t, d2)]
    d8 = [_bdot(x, x) for x in d4]
    t = [x + _bdot(x, y) for x, y in zip(t, d4)]
    t = [x + _bdot(x, y) for x, y in zip(t, d8)]
    for lower, upper in ((same16, same32), (same32, None)):
        off = ~lower if upper is None else upper & ~lower
        e = [jnp.where(off, a, 0.0) for a in a_list]
        tb = [x.astype(BF16) for x in t]
        te = [_bdot(x, y) for x, y in zip(tb, e)]
        t = [x - _bdot(y, z) for x, y, z in zip(t, te, tb)]
    return t


def _gdn_kernel(q_ref, k_ref, v_ref, qh_ref, kh_ref, vh_ref, wq_ref, wk_ref, wv_ref,
                beta_ref, gc_ref, gtail_ref, gcrow_ref, z_ref, ng_ref, o_ref,
                state_ref, qx_ref, kx_ref, vx_ref):
    hg = pl.program_id(1)
    t = pl.program_id(2)
    n = q_ref.shape[0]

    @pl.when(t == 0)
    def _():
        state_ref[...] = jnp.zeros_like(state_ref)

    def conv_silu(x_ref, halo_ref, w_ref, xx_ref):
        xx = jnp.concatenate([jnp.where(t > 0, halo_ref[...], 0.0), x_ref[...]], axis=0)
        acc = w_ref[CONV_K - 1:CONV_K, :] * xx[HALO:]
        for back in range(1, CONV_K):
            tap = CONV_K - 1 - back
            acc = acc + w_ref[tap:tap + 1, :] * pltpu.roll(xx, back, axis=0)[HALO:]
        return _silu(acc)

    q_all = conv_silu(q_ref, qh_ref, wq_ref, qx_ref)
    k_all = conv_silu(k_ref, kh_ref, wk_ref, kx_ref)
    v_all = conv_silu(v_ref, vh_ref, wv_ref, vx_ref)

    lane = lax.broadcasted_iota(jnp.int32, (n, 128), 1)

    def pick_lane(ref, idx):
        return jnp.sum(jnp.where(lane == idx, ref[...], 0.0), axis=-1, keepdims=True)

    row = lax.broadcasted_iota(jnp.int32, (n, n), 0)
    col = lax.broadcasted_iota(jnp.int32, (n, n), 1)
    same_chunk = (row // CHUNK) == (col // CHUNK)
    causal = same_chunk & (row >= col)
    strict = same_chunk & (row > col)
    same16 = (row // 16) == (col // 16)
    same32 = (row // 32) == (col // 32)

    heads = range(HEADS_PER_STEP)
    lanes = [slice(hl * DK, (hl + 1) * DK) for hl in heads]
    head = [hg * HEADS_PER_STEP + hl for hl in heads]
    q = [q_all[:, s] for s in lanes]
    k = [k_all[:, s] for s in lanes]
    v = [v_all[:, s] for s in lanes]
    q = [x * (lax.rsqrt(jnp.sum(x * x, axis=-1, keepdims=True) + EPS) * (DK ** -0.5)) for x in q]
    k = [x * lax.rsqrt(jnp.sum(x * x, axis=-1, keepdims=True) + EPS) for x in k]
    beta = [pick_lane(beta_ref, h) for h in head]
    gc = [pick_lane(gc_ref, ALPHA_LANE + h) for h in head]
    gtail = [pick_lane(gtail_ref, ALPHA_LANE + h) for h in head]
    gcrow = [gcrow_ref[pl.ds(h, 1), :] for h in head]
    kb = [x * b for x, b in zip(k, beta)]
    gram = [_bdot_nt(jnp.concatenate([x, y], axis=0), z) for x, y, z in zip(kb, q, k)]
    decay = [jnp.exp(jnp.where(causal, c - r, -1e30)) for c, r in zip(gc, gcrow)]
    a = [jnp.where(strict, g[:n] * d, 0.0) for g, d in zip(gram, decay)]
    attn = [(g[n:] * d).astype(BF16) for g, d in zip(gram, decay)]
    tinv = _unit_lower_inverse(a, same16, same32)
    eg = [jnp.exp(c) for c in gc]
    rhs = [jnp.concatenate([x * b, y * e], axis=1) for x, b, y, e in zip(v, beta, kb, eg)]
    sol = [_bdot(x, y) for x, y in zip(tinv, rhs)]
    u = [x[:, :DV] for x in sol]
    w = [x[:, DV:].astype(BF16) for x in sol]
    qd = [(x * e).astype(BF16) for x, e in zip(q, eg)]
    kt = [(x * jnp.exp(g)).astype(BF16) for x, g in zip(k, gtail)]
    state = [state_ref[hl] for hl in heads]
    v_new = [[] for _ in heads]
    o_state = [[] for _ in heads]
    for c in range(n // CHUNK):
        rows = slice(c * CHUNK, (c + 1) * CHUNK)
        last = (c + 1) * CHUNK - 1
        lhs = [jnp.concatenate([x[rows], y[rows]], axis=0) for x, y in zip(w, qd)]
        ws = [_bdot(x, s) for x, s in zip(lhs, state)]
        vn = [x[rows] - y[:CHUNK] for x, y in zip(u, ws)]
        for hl in heads:
            v_new[hl].append(vn[hl])
            o_state[hl].append(ws[hl][CHUNK:])
        chunk_decay = [jnp.exp(g[last:last + 1, :]) for g in gc]
        state = [s * d + _bdot_tn(x[rows], y) for s, d, x, y in zip(state, chunk_decay, kt, vn)]
    for hl in heads:
        state_ref[hl] = state[hl]
    o = [jnp.concatenate(x, axis=0) + _bdot(y, jnp.concatenate(z, axis=0))
         for x, y, z in zip(o_state, attn, v_new)]
    for hl in heads:
        y = o[hl]
        y = y * lax.rsqrt(jnp.mean(y * y, axis=-1, keepdims=True) + EPS) * ng_ref[...]
        o_ref[:, lanes[hl]] = (y * _silu(z_ref[:, lanes[hl]])).astype(o_ref.dtype)


def _gated_deltanet(proj_main, conv_w, beta, gc, gtail, gcrow, norm_g, batch, seq):
    width = HEADS_PER_STEP * DK
    nt = seq // SLAB
    halo_per_slab = SLAB // HALO

    def rows(b, h, t):
        return b * nt + t

    def halo_rows(b, h, t):
        return jnp.maximum((b * nt + t) * halo_per_slab - 1, 0)

    def sect(col0):
        base = col0 // width
        return (pl.BlockSpec((SLAB, width), lambda b, h, t: (rows(b, h, t), base + h)),
                pl.BlockSpec((HALO, width), lambda b, h, t: (halo_rows(b, h, t), base + h)))

    q_spec, qh_spec = sect(COL_QB)
    k_spec, kh_spec = sect(COL_KB)
    v_spec, vh_spec = sect(COL_VB)
    z_spec, _ = sect(COL_ZB)

    def conv_spec(col0):
        base = col0 // width
        return pl.BlockSpec((CONV_K, width), lambda b, h, t: (0, base + h))

    col_spec = pl.BlockSpec((SLAB, 128), lambda b, h, t: (rows(b, h, t), 0))
    return pl.pallas_call(
        _gdn_kernel,
        grid=(batch, H_B // HEADS_PER_STEP, nt),
        in_specs=[q_spec, k_spec, v_spec, qh_spec, kh_spec, vh_spec,
                  conv_spec(0), conv_spec(H_B * DK), conv_spec(2 * H_B * DK),
                  col_spec, col_spec, col_spec,
                  pl.BlockSpec((H_B, SLAB), lambda b, h, t: (0, rows(b, h, t))),
                  z_spec,
                  pl.BlockSpec((1, DV), lambda b, h, t: (0, 0))],
        out_specs=pl.BlockSpec((SLAB, width), lambda b, h, t: (rows(b, h, t), h)),
        out_shape=jax.ShapeDtypeStruct((batch * seq, WIDTH_B), BF16),
        scratch_shapes=[pltpu.VMEM((HEADS_PER_STEP, DK, DV), F32),
                        pltpu.VMEM((HALO + SLAB, width), F32),
                        pltpu.VMEM((HALO + SLAB, width), F32),
                        pltpu.VMEM((HALO + SLAB, width), F32)],
        compiler_params=pltpu.CompilerParams(
            dimension_semantics=("parallel", "parallel", "arbitrary"),
            vmem_limit_bytes=VMEM_LIMIT),
        name="gated_deltanet",
    )(proj_main, proj_main, proj_main, proj_main, proj_main, proj_main,
      conv_w, conv_w, conv_w, beta, gc, gtail, gcrow, proj_main, norm_g.reshape(1, DV))


def _merge_kernel(oa_ref, ob_ref, wa_ref, wb_ref, ga_ref, gb_ref, o_ref):
    ya = jnp.dot(oa_ref[...], wa_ref[...], preferred_element_type=F32)
    yb = jnp.dot(ob_ref[...], wb_ref[...], preferred_element_type=F32)
    merged = jax.nn.sigmoid(ga_ref[...]) * ya + jax.nn.sigmoid(gb_ref[...]) * yb
    o_ref[...] = merged.astype(o_ref.dtype)


def _merge(o_a, o_b, w_pa, w_pb, layer, gates, tm, tn):
    m = o_a.shape[0]
    gb_blk = D_MODEL // tn
    return pl.pallas_call(
        _merge_kernel,
        grid=(m // tm, D_MODEL // tn),
        in_specs=[pl.BlockSpec((tm, WIDTH_A), lambda i, j: (i, 0)),
                  pl.BlockSpec((tm, WIDTH_B), lambda i, j: (i, 0)),
                  pl.BlockSpec((None, WIDTH_A, tn), lambda i, j: (layer, 0, j)),
                  pl.BlockSpec((None, WIDTH_B, tn), lambda i, j: (layer, 0, j)),
                  pl.BlockSpec((tm, tn), lambda i, j: (i, j)),
                  pl.BlockSpec((tm, tn), lambda i, j: (i, gb_blk + j))],
        out_specs=pl.BlockSpec((tm, tn), lambda i, j: (i, j)),
        out_shape=jax.ShapeDtypeStruct((m, D_MODEL), BF16),
        compiler_params=pltpu.CompilerParams(dimension_semantics=("parallel", "parallel"),
                                             vmem_limit_bytes=VMEM_LIMIT),
        name="branch_merge",
    )(o_a, o_b, w_pa, w_pb, gates, gates)


def _main_src_tile(j):
    last = COL_KA // W_TILE
    return jnp.where(j < SRC_KV_TILE, j, jnp.where(j < last, j + 1, jnp.where(j == last, SRC_KV_TILE, j)))


def _layer(x2, h, batch, seq, layer, wb_main, wb_gates, sinks, conv_w, a_log, dt_bias, gdn_g,
           wb_pa, wb_pb, wb_o, post_g, next_pre_g):
    proj = _matmul(h, wb_main, layer, 1024, 1024, F32, "in_proj_main")
    gates = _matmul(h, wb_gates, layer, 1024, 1024, F32, "in_proj_gates")
    o_a = _attention(proj, sinks, batch, seq)
    alpha_t = proj[:, COL_BA + ALPHA_LANE:COL_BA + ALPHA_LANE + H_B].T
    beta, gc, gtail, gcrow = _gates(proj, alpha_t, a_log, dt_bias)
    o_b = _gated_deltanet(proj, conv_w, beta, gc, gtail, gcrow, gdn_g, batch, seq)
    merged = _merge(o_a, o_b, wb_pa, wb_pb, layer, gates, 1024, 512)
    out = _matmul(merged, wb_o, layer, 1024, 1024, F32, "out_proj")
    return _postnorm_residual(x2, out, post_g, next_pre_g, 256)


def kernel(x, pre_norm_g, w_in, sinks, conv_w, a_log, dt_bias, gdn_norm_g, w_pa, w_pb, w_o, post_norm_g):
    batch, seq, d = x.shape
    depth = w_in.shape[0]
    x2 = x.reshape(batch * seq, d)
    wb_main = _cast_tiles(w_in, N_MAIN // W_TILE, _main_src_tile, "cast_in_weights")
    wb_gates = _cast_gate_tiles(w_in)
    identity = lambda j: j
    wb_pa = _cast_tiles(w_pa, D_MODEL // W_TILE, identity, "cast_pa_weights")
    wb_pb = _cast_tiles(w_pb, D_MODEL // W_TILE, identity, "cast_pb_weights")
    wb_o = _cast_tiles(w_o, D_MODEL // W_TILE, identity, "cast_out_weights")
    h = _rmsnorm(x2, pre_norm_g[0], 256)
    for layer in range(depth):
        next_pre_g = pre_norm_g[layer + 1] if layer + 1 < depth else None
        x2, h = _layer(x2, h, batch, seq, layer, wb_main, wb_gates, sinks[layer], conv_w[layer],
                       a_log[layer], dt_bias[layer], gdn_norm_g[layer], wb_pa, wb_pb, wb_o,
                       post_norm_g[layer], next_pre_g)
    return x2.reshape(batch, seq, d)
```

```python
import functools

import jax
import jax.numpy as jnp
from jax import lax
from jax.experimental import pallas as pl
from jax.experimental.pallas import tpu as pltpu

F32 = jnp.float32
BF16 = jnp.bfloat16

D_MODEL = 4096
HD_A = 64
H_A = 32
H_KV = 4
GROUP = H_A // H_KV
WIDTH_A = H_A * HD_A
WINDOW = 128
H_B = 16
DK = 128
DV = 128
WIDTH_B = H_B * DV
CONV_K = 4
CHUNK = 64
EPS = 1e-6

SLAB = 256
HALO = 8
HEADS_PER_STEP = 8
ALPHA_LANE = H_B

VMEM_LIMIT = 56 * 1024 * 1024

COL_QA = 0
COL_ZA = COL_QA + WIDTH_A
COL_QB = COL_ZA + WIDTH_A
COL_KB = COL_QB + H_B * DK
COL_VB = COL_KB + H_B * DK
COL_ZB = COL_VB + H_B * DV
KV_A = H_KV * HD_A
COL_KA = COL_ZB + WIDTH_B
COL_VA = COL_KA + KV_A
COL_BA = COL_VA + KV_A
W_TILE = 512
N_MAIN = COL_BA + W_TILE
SRC_KV_TILE = WIDTH_A // W_TILE
SRC_BA = 2 * WIDTH_A + 2 * KV_A + 2 * H_B * DK + H_B * DV + WIDTH_B
GATE_SHIFT = 2 * H_B
LANE = 128


def _bdot(a, b):
    return jnp.dot(a.astype(BF16), b.astype(BF16), preferred_element_type=F32)


def _bdot_nt(a, b):
    return lax.dot_general(a.astype(BF16), b.astype(BF16), (((1,), (1,)), ((), ())),
                           preferred_element_type=F32)


def _bdot_tn(a, b):
    return lax.dot_general(a.astype(BF16), b.astype(BF16), (((0,), (0,)), ((), ())),
                           preferred_element_type=F32)


def _silu(x):
    return x * jax.nn.sigmoid(x)


def _rmsnorm_kernel(x_ref, g_ref, o_ref):
    x = x_ref[...]
    ms = jnp.mean(x * x, axis=-1, keepdims=True)
    o_ref[...] = (x * lax.rsqrt(ms + EPS) * g_ref[...]).astype(o_ref.dtype)


def _rmsnorm(x, g, rows):
    m, d = x.shape
    return pl.pallas_call(
        _rmsnorm_kernel,
        grid=(m // rows,),
        in_specs=[pl.BlockSpec((rows, d), lambda i: (i, 0)),
                  pl.BlockSpec((1, d), lambda i: (0, 0))],
        out_specs=pl.BlockSpec((rows, d), lambda i: (i, 0)),
        out_shape=jax.ShapeDtypeStruct((m, d), BF16),
        compiler_params=pltpu.CompilerParams(dimension_semantics=("parallel",),
                                             vmem_limit_bytes=VMEM_LIMIT),
        name="pre_norm",
    )(x, g.reshape(1, d))


def _postnorm_kernel(x_ref, y_ref, g_ref, o_ref):
    y = y_ref[...]
    ms = jnp.mean(y * y, axis=-1, keepdims=True)
    o_ref[...] = x_ref[...] + y * lax.rsqrt(ms + EPS) * g_ref[...]


def _postnorm_prenorm_kernel(x_ref, y_ref, g_ref, gn_ref, o_ref, h_ref):
    y = y_ref[...]
    ms = jnp.mean(y * y, axis=-1, keepdims=True)
    x = x_ref[...] + y * lax.rsqrt(ms + EPS) * g_ref[...]
    o_ref[...] = x
    ms = jnp.mean(x * x, axis=-1, keepdims=True)
    h_ref[...] = (x * lax.rsqrt(ms + EPS) * gn_ref[...]).astype(h_ref.dtype)


def _postnorm_residual(x, y, g, next_pre_g, rows):
    m, d = x.shape
    row_spec = pl.BlockSpec((rows, d), lambda i: (i, 0))
    gain_spec = pl.BlockSpec((1, d), lambda i: (0, 0))
    params = pltpu.CompilerParams(dimension_semantics=("parallel",), vmem_limit_bytes=VMEM_LIMIT)
    if next_pre_g is None:
        out = pl.pallas_call(
            _postnorm_kernel, grid=(m // rows,),
            in_specs=[row_spec, row_spec, gain_spec], out_specs=row_spec,
            out_shape=jax.ShapeDtypeStruct((m, d), F32),
            compiler_params=params, name="post_norm",
        )(x, y, g.reshape(1, d))
        return out, None
    return pl.pallas_call(
        _postnorm_prenorm_kernel, grid=(m // rows,),
        in_specs=[row_spec, row_spec, gain_spec, gain_spec], out_specs=[row_spec, row_spec],
        out_shape=[jax.ShapeDtypeStruct((m, d), F32), jax.ShapeDtypeStruct((m, d), BF16)],
        compiler_params=params, name="post_pre_norm",
    )(x, y, g.reshape(1, d), next_pre_g.reshape(1, d))


def _matmul_kernel(a_ref, b_ref, o_ref):
    o_ref[...] = jnp.dot(a_ref[...], b_ref[...], preferred_element_type=F32).astype(o_ref.dtype)


def _matmul(a, b_stack, layer, tm, tn, out_dtype, name):
    m, k = a.shape
    _, _, n = b_stack.shape
    return pl.pallas_call(
        _matmul_kernel,
        grid=(m // tm, n // tn),
        in_specs=[pl.BlockSpec((tm, k), lambda i, j: (i, 0)),
                  pl.BlockSpec((None, k, tn), lambda i, j: (layer, 0, j))],
        out_specs=pl.BlockSpec((tm, tn), lambda i, j: (i, j)),
        out_shape=jax.ShapeDtypeStruct((m, n), out_dtype),
        compiler_params=pltpu.CompilerParams(dimension_semantics=("parallel", "parallel"),
                                             vmem_limit_bytes=VMEM_LIMIT),
        name=name,
    )(a, b_stack)


def _matmul_nt_kernel(a_ref, bt_ref, o_ref):
    o_ref[...] = _bdot_nt(a_ref[...], bt_ref[...]).astype(o_ref.dtype)


def _matmul_nt(a, bt_stack, layer, tm, tn, out_dtype, name):
    m, k = a.shape
    _, n, _ = bt_stack.shape
    return pl.pallas_call(
        _matmul_nt_kernel,
        grid=(m // tm, n // tn),
        in_specs=[pl.BlockSpec((tm, k), lambda i, j: (i, 0)),
                  pl.BlockSpec((None, tn, k), lambda i, j: (layer, j, 0))],
        out_specs=pl.BlockSpec((tm, tn), lambda i, j: (i, j)),
        out_shape=jax.ShapeDtypeStruct((m, n), out_dtype),
        compiler_params=pltpu.CompilerParams(dimension_semantics=("parallel", "parallel"),
                                             vmem_limit_bytes=VMEM_LIMIT),
        name=name,
    )(a, bt_stack)


def _cast_tiles_kernel(w_ref, o_ref):
    o_ref[...] = w_ref[...].astype(o_ref.dtype)


def _cast_tiles(w_stack, name):
    depth, k, n = w_stack.shape
    spec = pl.BlockSpec((None, k, W_TILE), lambda l, j: (l, 0, j))
    return pl.pallas_call(
        _cast_tiles_kernel,
        grid=(depth, n // W_TILE),
        in_specs=[spec],
        out_specs=spec,
        out_shape=jax.ShapeDtypeStruct((depth, k, n), BF16),
        compiler_params=pltpu.CompilerParams(dimension_semantics=("parallel", "parallel"),
                                             vmem_limit_bytes=VMEM_LIMIT),
        name=name,
    )(w_stack)


def _cast_row_tiles(wt_stack, n_tiles, src_tile, name):
    depth, _, k = wt_stack.shape
    return pl.pallas_call(
        _cast_tiles_kernel,
        grid=(depth, n_tiles),
        in_specs=[pl.BlockSpec((None, W_TILE, k), lambda l, j: (l, src_tile(j), 0))],
        out_specs=pl.BlockSpec((None, W_TILE, k), lambda l, j: (l, j, 0)),
        out_shape=jax.ShapeDtypeStruct((depth, n_tiles * W_TILE, k), BF16),
        compiler_params=pltpu.CompilerParams(dimension_semantics=("parallel", "parallel"),
                                             vmem_limit_bytes=VMEM_LIMIT),
        name=name,
    )(wt_stack)


def _cast_shifted_rows_kernel(a_ref, b_ref, o_ref):
    o_ref[0:W_TILE - GATE_SHIFT, :] = a_ref[GATE_SHIFT:, :].astype(o_ref.dtype)
    o_ref[W_TILE - GATE_SHIFT:, :] = b_ref[...].astype(o_ref.dtype)


def _cast_gate_rows(wt_stack):
    depth, _, k = wt_stack.shape
    first = SRC_BA // W_TILE
    n_tiles = 2 * D_MODEL // W_TILE
    shifts_per_tile = W_TILE // GATE_SHIFT
    return pl.pallas_call(
        _cast_shifted_rows_kernel,
        grid=(depth, n_tiles),
        in_specs=[pl.BlockSpec((None, W_TILE, k), lambda l, j: (l, first + j, 0)),
                  pl.BlockSpec((None, GATE_SHIFT, k), lambda l, j: (l, (first + j + 1) * shifts_per_tile, 0))],
        out_specs=pl.BlockSpec((None, W_TILE, k), lambda l, j: (l, j, 0)),
        out_shape=jax.ShapeDtypeStruct((depth, n_tiles * W_TILE, k), BF16),
        compiler_params=pltpu.CompilerParams(dimension_semantics=("parallel", "parallel"),
                                             vmem_limit_bytes=VMEM_LIMIT),
        name="cast_gate_weights",
    )(wt_stack, wt_stack)


def _attn_kernel(sinks_ref, q_ref, z_ref, kc_ref, kp_ref, vc_ref, vp_ref, o_ref):
    n = pl.program_id(1)
    blk = q_ref.shape[0]
    qi = lax.broadcasted_iota(jnp.int32, (blk, 2 * blk), 0)
    kj = lax.broadcasted_iota(jnp.int32, (blk, 2 * blk), 1)
    dist = qi + blk - kj
    valid = (dist >= 0) & (dist < WINDOW) & ((kj >= blk) | (n > 0))
    distf = dist.astype(F32)
    kband = jnp.concatenate([kp_ref[...], kc_ref[...]], axis=0).astype(BF16)
    vband = jnp.concatenate([vp_ref[...], vc_ref[...]], axis=0).astype(BF16)
    scale = HD_A ** -0.5
    for h in range(H_KV):
        kh = kband[:, h * HD_A:(h + 1) * HD_A]
        vh = vband[:, h * HD_A:(h + 1) * HD_A]
        heads = [h * GROUP + g for g in range(GROUP)]
        cols = [slice(hh * HD_A, (hh + 1) * HD_A) for hh in heads]
        s = [_bdot_nt(q_ref[:, c], kh) for c in cols]
        p, denom = [], []
        for hh, s_h in zip(heads, s):
            slope = 2.0 ** (-8.0 * (hh + 1) / H_A)
            s_h = jnp.where(valid, s_h * scale - slope * distf, -1e30)
            sink = sinks_ref[hh]
            m = jnp.maximum(jnp.max(s_h, axis=-1, keepdims=True), sink)
            p_h = jnp.exp(s_h - m)
            denom.append(jnp.sum(p_h, axis=-1, keepdims=True) + jnp.exp(sink - m))
            p.append(p_h.astype(BF16))
        o = _bdot(jnp.concatenate(p, axis=0), vh)
        for g, c in enumerate(cols):
            o_h = o[g * blk:(g + 1) * blk] / denom[g]
            o_ref[:, c] = (o_h * _silu(z_ref[:, c])).astype(o_ref.dtype)


def _attention(proj, sinks, batch, seq):
    nb = seq // WINDOW
    k_blk = COL_KA // KV_A
    v_blk = COL_VA // KV_A

    def row(b, n):
        return b * nb + n

    def prev_row(b, n):
        return b * nb + jnp.maximum(n - 1, 0)

    return pl.pallas_call(
        _attn_kernel,
        grid=(batch, nb),
        in_specs=[
            pl.BlockSpec(memory_space=pltpu.SMEM),
            pl.BlockSpec((WINDOW, WIDTH_A), lambda b, n: (row(b, n), COL_QA // WIDTH_A)),
            pl.BlockSpec((WINDOW, WIDTH_A), lambda b, n: (row(b, n), COL_ZA // WIDTH_A)),
            pl.BlockSpec((WINDOW, KV_A), lambda b, n: (row(b, n), k_blk)),
            pl.BlockSpec((WINDOW, KV_A), lambda b, n: (prev_row(b, n), k_blk)),
            pl.BlockSpec((WINDOW, KV_A), lambda b, n: (row(b, n), v_blk)),
            pl.BlockSpec((WINDOW, KV_A), lambda b, n: (prev_row(b, n), v_blk)),
        ],
        out_specs=pl.BlockSpec((WINDOW, WIDTH_A), lambda b, n: (row(b, n), 0)),
        out_shape=jax.ShapeDtypeStruct((batch * seq, WIDTH_A), BF16),
        compiler_params=pltpu.CompilerParams(dimension_semantics=("parallel", "arbitrary"),
                                             vmem_limit_bytes=VMEM_LIMIT),
        name="swa_attention",
    )(sinks, proj, proj, proj, proj, proj, proj)


def _softplus(x):
    return jnp.maximum(x, 0.0) + jnp.log1p(jnp.exp(-jnp.abs(x)))


def _chunk_scans(g, axis):
    size = g.shape[axis]
    pos = lax.broadcasted_iota(jnp.int32, g.shape, axis) % CHUNK
    pre = g
    suf = g
    step = 1
    while step < CHUNK:
        pre = pre + jnp.where(pos >= step, pltpu.roll(pre, step, axis=axis), 0.0)
        suf = suf + jnp.where(pos < CHUNK - step, pltpu.roll(suf, size - step, axis=axis), 0.0)
        step *= 2
    return pre, suf - g


def _gate_kernel(ba_ref, alt_ref, alog_row_ref, dtb_row_ref, alog_col_ref, dtb_col_ref,
                 beta_ref, gc_ref, gtail_ref, gcrow_ref):
    ba = ba_ref[...]
    beta_ref[...] = jax.nn.sigmoid(ba)
    g = -jnp.exp(alog_row_ref[...]) * _softplus(ba + dtb_row_ref[...])
    gc, gtail = _chunk_scans(g, 0)
    gc_ref[...] = gc
    gtail_ref[...] = gtail
    g_row = -jnp.exp(alog_col_ref[...]) * _softplus(alt_ref[...] + dtb_col_ref[...])
    gcrow_ref[...] = _chunk_scans(g_row, 1)[0]


def _gates(proj, alpha_t, a_log, dt_bias):
    rows = proj.shape[0]
    lane_pad = (ALPHA_LANE, 128 - ALPHA_LANE - H_B)
    alog_row = jnp.pad(a_log, lane_pad).reshape(1, 128)
    dtb_row = jnp.pad(dt_bias, lane_pad).reshape(1, 128)
    ba_blk = COL_BA // 128
    col_out = jax.ShapeDtypeStruct((rows, 128), F32)
    col_spec = pl.BlockSpec((SLAB, 128), lambda i: (i, 0))
    small = lambda shape: pl.BlockSpec(shape, lambda i: (0, 0))
    return pl.pallas_call(
        _gate_kernel,
        grid=(rows // SLAB,),
        in_specs=[pl.BlockSpec((SLAB, 128), lambda i: (i, ba_blk)),
                  pl.BlockSpec((H_B, SLAB), lambda i: (0, i)),
                  small((1, 128)), small((1, 128)), small((H_B, 1)), small((H_B, 1))],
        out_specs=[col_spec, col_spec, col_spec, pl.BlockSpec((H_B, SLAB), lambda i: (0, i))],
        out_shape=[col_out, col_out, col_out, jax.ShapeDtypeStruct((H_B, rows), F32)],
        compiler_params=pltpu.CompilerParams(dimension_semantics=("parallel",),
                                             vmem_limit_bytes=VMEM_LIMIT),
        name="gdn_gates",
    )(proj, alpha_t, alog_row, dtb_row, a_log.reshape(H_B, 1), dt_bias.reshape(H_B, 1))


def _unit_lower_inverse(a_list, same16, same32):
    n = a_list[0].shape[0]
    eye = (lax.broadcasted_iota(jnp.int32, (n, n), 0) ==
           lax.broadcasted_iota(jnp.int32, (n, n), 1)).astype(F32)
    d = [jnp.where(same16, a, 0.0) for a in a_list]
    db = [x.astype(BF16) for x in d]
    d2 = [_bdot(x, x).astype(BF16) for x in db]
    t = [eye - x for x in d]
    d4 = [_bdot(x, x).astype(BF16) for x in d2]
    t = [x + _bdot(x, y) for x, y in zip(t, d2)]
    d8 = [_bdot(x, x) for x in d4]
    t = [x + _bdot(x, y) for x, y in zip(t, d4)]
    t = [x + _bdot(x, y) for x, y in zip(t, d8)]
    for lower, upper in ((same16, same32), (same32, None)):
        off = ~lower if upper is None else upper & ~lower
        e = [jnp.where(off, a, 0.0) for a in a_list]
        tb = [x.astype(BF16) for x in t]
        te = [_bdot(x, y) for x, y in zip(tb, e)]
        t = [x - _bdot(y, z) for x, y, z in zip(t, te, tb)]
    return t


def _gdn_kernel(q_ref, k_ref, v_ref, qh_ref, kh_ref, vh_ref, wq_ref, wk_ref, wv_ref,
                beta_ref, gc_ref, gtail_ref, gcrow_ref, z_ref, ng_ref, o_ref,
                state_ref, qx_ref, kx_ref, vx_ref):
    hg = pl.program_id(1)
    t = pl.program_id(2)
    n = q_ref.shape[0]

    @pl.when(t == 0)
    def _():
        state_ref[...] = jnp.zeros_like(state_ref)

    def conv_silu(x_ref, halo_ref, w_ref, xx_ref):
        xx = jnp.concatenate([jnp.where(t > 0, halo_ref[...], 0.0), x_ref[...]], axis=0)
        acc = w_ref[CONV_K - 1:CONV_K, :] * xx[HALO:]
        for back in range(1, CONV_K):
            tap = CONV_K - 1 - back
            acc = acc + w_ref[tap:tap + 1, :] * pltpu.roll(xx, back, axis=0)[HALO:]
        return _silu(acc)

    q_all = conv_silu(q_ref, qh_ref, wq_ref, qx_ref)
    k_all = conv_silu(k_ref, kh_ref, wk_ref, kx_ref)
    v_all = conv_silu(v_ref, vh_ref, wv_ref, vx_ref)

    lane = lax.broadcasted_iota(jnp.int32, (n, 128), 1)

    def pick_lane(ref, idx):
        return jnp.sum(jnp.where(lane == idx, ref[...], 0.0), axis=-1, keepdims=True)

    row = lax.broadcasted_iota(jnp.int32, (n, n), 0)
    col = lax.broadcasted_iota(jnp.int32, (n, n), 1)
    same_chunk = (row // CHUNK) == (col // CHUNK)
    causal = same_chunk & (row >= col)
    strict = same_chunk & (row > col)
    same16 = (row // 16) == (col // 16)
    same32 = (row // 32) == (col // 32)

    heads = range(HEADS_PER_STEP)
    lanes = [slice(hl * DK, (hl + 1) * DK) for hl in heads]
    head = [hg * HEADS_PER_STEP + hl for hl in heads]
    q = [q_all[:, s] for s in lanes]
    k = [k_all[:, s] for s in lanes]
    v = [v_all[:, s] for s in lanes]
    q = [x * (lax.rsqrt(jnp.sum(x * x, axis=-1, keepdims=True) + EPS) * (DK ** -0.5)) for x in q]
    k = [x * lax.rsqrt(jnp.sum(x * x, axis=-1, keepdims=True) + EPS) for x in k]
    beta = [pick_lane(beta_ref, h) for h in head]
    gc = [pick_lane(gc_ref, ALPHA_LANE + h) for h in head]
    gtail = [pick_lane(gtail_ref, ALPHA_LANE + h) for h in head]
    gcrow = [gcrow_ref[pl.ds(h, 1), :] for h in head]
    kb = [x * b for x, b in zip(k, beta)]
    gram = [_bdot_nt(jnp.concatenate([x, y], axis=0), z) for x, y, z in zip(kb, q, k)]
    decay = [jnp.exp(jnp.where(causal, c - r, -1e30)) for c, r in zip(gc, gcrow)]
    a = [jnp.where(strict, g[:n] * d, 0.0) for g, d in zip(gram, decay)]
    attn = [(g[n:] * d).astype(BF16) for g, d in zip(gram, decay)]
    tinv = _unit_lower_inverse(a, same16, same32)
    eg = [jnp.exp(c) for c in gc]
    rhs = [jnp.concatenate([x * b, y * e], axis=1) for x, b, y, e in zip(v, beta, kb, eg)]
    sol = [_bdot(x, y) for x, y in zip(tinv, rhs)]
    u = [x[:, :DV] for x in sol]
    w = [x[:, DV:].astype(BF16) for x in sol]
    qd = [(x * e).astype(BF16) for x, e in zip(q, eg)]
    kt = [(x * jnp.exp(g)).astype(BF16) for x, g in zip(k, gtail)]
    state = [state_ref[hl] for hl in heads]
    v_new = [[] for _ in heads]
    o_state = [[] for _ in heads]
    for c in range(n // CHUNK):
        rows = slice(c * CHUNK, (c + 1) * CHUNK)
        last = (c + 1) * CHUNK - 1
        lhs = [jnp.concatenate([x[rows], y[rows]], axis=0) for x, y in zip(w, qd)]
        ws = [_bdot(x, s) for x, s in zip(lhs, state)]
        vn = [x[rows] - y[:CHUNK] for x, y in zip(u, ws)]
        for hl in heads:
            v_new[hl].append(vn[hl])
            o_state[hl].append(ws[hl][CHUNK:])
        chunk_decay = [jnp.exp(g[last:last + 1, :]) for g in gc]
        state = [s * d + _bdot_tn(x[rows], y) for s, d, x, y in zip(state, chunk_decay, kt, vn)]
    for hl in heads:
        state_ref[hl] = state[hl]
    o = [jnp.concatenate(x, axis=0) + _bdot(y, jnp.concatenate(z, axis=0))
         for x, y, z in zip(o_state, attn, v_new)]
    for hl in heads:
        y = o[hl]
        y = y * lax.rsqrt(jnp.mean(y * y, axis=-1, keepdims=True) + EPS) * ng_ref[...]
        o_ref[:, lanes[hl]] = (y * _silu(z_ref[:, lanes[hl]])).astype(o_ref.dtype)


def _gated_deltanet(proj_main, conv_w, beta, gc, gtail, gcrow, norm_g, batch, seq):
    width = HEADS_PER_STEP * DK
    nt = seq // SLAB
    halo_per_slab = SLAB // HALO

    def rows(b, h, t):
        return b * nt + t

    def halo_rows(b, h, t):
        return jnp.maximum((b * nt + t) * halo_per_slab - 1, 0)

    def sect(col0):
        base = col0 // width
        return (pl.BlockSpec((SLAB, width), lambda b, h, t: (rows(b, h, t), base + h)),
                pl.BlockSpec((HALO, width), lambda b, h, t: (halo_rows(b, h, t), base + h)))

    q_spec, qh_spec = sect(COL_QB)
    k_spec, kh_spec = sect(COL_KB)
    v_spec, vh_spec = sect(COL_VB)
    z_spec, _ = sect(COL_ZB)

    def conv_spec(col0):
        base = col0 // width
        return pl.BlockSpec((CONV_K, width), lambda b, h, t: (0, base + h))

    col_spec = pl.BlockSpec((SLAB, 128), lambda b, h, t: (rows(b, h, t), 0))
    return pl.pallas_call(
        _gdn_kernel,
        grid=(batch, H_B // HEADS_PER_STEP, nt),
        in_specs=[q_spec, k_spec, v_spec, qh_spec, kh_spec, vh_spec,
                  conv_spec(0), conv_spec(H_B * DK), conv_spec(2 * H_B * DK),
                  col_spec, col_spec, col_spec,
                  pl.BlockSpec((H_B, SLAB), lambda b, h, t: (0, rows(b, h, t))),
                  z_spec,
                  pl.BlockSpec((1, DV), lambda b, h, t: (0, 0))],
        out_specs=pl.BlockSpec((SLAB, width), lambda b, h, t: (rows(b, h, t), h)),
        out_shape=jax.ShapeDtypeStruct((batch * seq, WIDTH_B), BF16),
        scratch_shapes=[pltpu.VMEM((HEADS_PER_STEP, DK, DV), F32),
                        pltpu.VMEM((HALO + SLAB, width), F32),
                        pltpu.VMEM((HALO + SLAB, width), F32),
                        pltpu.VMEM((HALO + SLAB, width), F32)],
        compiler_params=pltpu.CompilerParams(
            dimension_semantics=("parallel", "parallel", "arbitrary"),
            vmem_limit_bytes=VMEM_LIMIT),
        name="gated_deltanet",
    )(proj_main, proj_main, proj_main, proj_main, proj_main, proj_main,
      conv_w, conv_w, conv_w, beta, gc, gtail, gcrow, proj_main, norm_g.reshape(1, DV))


def _merge_kernel(oa_ref, ob_ref, wa_ref, wb_ref, ga_ref, gb_ref, o_ref):
    ya = jnp.dot(oa_ref[...], wa_ref[...], preferred_element_type=F32)
    yb = jnp.dot(ob_ref[...], wb_ref[...], preferred_element_type=F32)
    merged = jax.nn.sigmoid(ga_ref[...]) * ya + jax.nn.sigmoid(gb_ref[...]) * yb
    o_ref[...] = merged.astype(o_ref.dtype)


def _merge(o_a, o_b, w_pa, w_pb, layer, gates, tm, tn):
    m = o_a.shape[0]
    gb_blk = D_MODEL // tn
    return pl.pallas_call(
        _merge_kernel,
        grid=(m // tm, D_MODEL // tn),
        in_specs=[pl.BlockSpec((tm, WIDTH_A), lambda i, j: (i, 0)),
                  pl.BlockSpec((tm, WIDTH_B), lambda i, j: (i, 0)),
                  pl.BlockSpec((None, WIDTH_A, tn), lambda i, j: (layer, 0, j)),
                  pl.BlockSpec((None, WIDTH_B, tn), lambda i, j: (layer, 0, j)),
                  pl.BlockSpec((tm, tn), lambda i, j: (i, j)),
                  pl.BlockSpec((tm, tn), lambda i, j: (i, gb_blk + j))],
        out_specs=pl.BlockSpec((tm, tn), lambda i, j: (i, j)),
        out_shape=jax.ShapeDtypeStruct((m, D_MODEL), BF16),
        compiler_params=pltpu.CompilerParams(dimension_semantics=("parallel", "parallel"),
                                             vmem_limit_bytes=VMEM_LIMIT),
        name="branch_merge",
    )(o_a, o_b, w_pa, w_pb, gates, gates)


def _main_src_tile(j):
    last = COL_KA // W_TILE
    return jnp.where(j < SRC_KV_TILE, j, jnp.where(j < last, j + 1, jnp.where(j == last, SRC_KV_TILE, j)))


def _layer(x2, h, batch, seq, layer, wb_main, wb_gates, sinks, conv_w, a_log, dt_bias, gdn_g,
           wb_pa, wb_pb, wb_o, post_g, next_pre_g):
    proj = _matmul_nt(h, wb_main, layer, 1024, 1024, F32, "in_proj_main")
    gates = _matmul_nt(h, wb_gates, layer, 1024, 1024, F32, "in_proj_gates")
    o_a = _attention(proj, sinks, batch, seq)
    alpha_t = proj[:, COL_BA + ALPHA_LANE:COL_BA + ALPHA_LANE + H_B].T
    beta, gc, gtail, gcrow = _gates(proj, alpha_t, a_log, dt_bias)
    o_b = _gated_deltanet(proj, conv_w, beta, gc, gtail, gcrow, gdn_g, batch, seq)
    merged = _merge(o_a, o_b, wb_pa, wb_pb, layer, gates, 1024, 512)
    out = _matmul(merged, wb_o, layer, 1024, 1024, F32, "out_proj")
    return _postnorm_residual(x2, out, post_g, next_pre_g, 256)


def kernel(x, pre_norm_g, w_in, sinks, conv_w, a_log, dt_bias, gdn_norm_g, w_pa, w_pb, w_o, post_norm_g):
    batch, seq, d = x.shape
    depth = w_in.shape[0]
    x2 = x.reshape(batch * seq, d)
    w_in_t = jnp.transpose(w_in, (0, 2, 1))
    wb_main = _cast_row_tiles(w_in_t, N_MAIN // W_TILE, _main_src_tile, "cast_in_weights")
    wb_gates = _cast_gate_rows(w_in_t)
    wb_pa = _cast_tiles(w_pa, "cast_pa_weights")
    wb_pb = _cast_tiles(w_pb, "cast_pb_weights")
    wb_o = _cast_tiles(w_o, "cast_out_weights")
    h = _rmsnorm(x2, pre_norm_g[0], 256)
    for layer in range(depth):
        next_pre_g = pre_norm_g[layer + 1] if layer + 1 < depth else None
        x2, h = _layer(x2, h, batch, seq, layer, wb_main, wb_gates, sinks[layer], conv_w[layer],
                       a_log[layer], dt_bias[layer], gdn_norm_g[layer], wb_pa, wb_pb, wb_o,
                       post_norm_g[layer], next_pre_g)
    return x2.reshape(batch, seq, d)
```

```python
import functools

import jax
import jax.numpy as jnp
from jax import lax
from jax.experimental import pallas as pl
from jax.experimental.pallas import tpu as pltpu

F32 = jnp.float32
BF16 = jnp.bfloat16

D_MODEL = 4096
HD_A = 64
H_A = 32
H_KV = 4
GROUP = H_A // H_KV
WIDTH_A = H_A * HD_A
WINDOW = 128
H_B = 16
DK = 128
DV = 128
WIDTH_B = H_B * DV
CONV_K = 4
CHUNK = 64
EPS = 1e-6

SLAB = 256
HALO = 8
HEADS_PER_STEP = 16
ALPHA_LANE = H_B

VMEM_LIMIT = 56 * 1024 * 1024

COL_QA = 0
COL_ZA = COL_QA + WIDTH_A
COL_QB = COL_ZA + WIDTH_A
COL_KB = COL_QB + H_B * DK
COL_VB = COL_KB + H_B * DK
COL_ZB = COL_VB + H_B * DV
KV_A = H_KV * HD_A
COL_KA = COL_ZB + WIDTH_B
COL_VA = COL_KA + KV_A
COL_BA = COL_VA + KV_A
W_TILE = 512
N_MAIN = COL_BA + W_TILE
SRC_KV_TILE = WIDTH_A // W_TILE
SRC_BA = 2 * WIDTH_A + 2 * KV_A + 2 * H_B * DK + H_B * DV + WIDTH_B
GATE_SHIFT = 2 * H_B
LANE = 128


def _bdot(a, b):
    return jnp.dot(a.astype(BF16), b.astype(BF16), preferred_element_type=F32)


def _bdot_nt(a, b):
    return lax.dot_general(a.astype(BF16), b.astype(BF16), (((1,), (1,)), ((), ())),
                           preferred_element_type=F32)


def _bdot_tn(a, b):
    return lax.dot_general(a.astype(BF16), b.astype(BF16), (((0,), (0,)), ((), ())),
                           preferred_element_type=F32)


def _silu(x):
    half = 0.5 * x
    return half + half * jnp.tanh(half)


def _rmsnorm_kernel(x_ref, g_ref, o_ref):
    x = x_ref[...]
    ms = jnp.mean(x * x, axis=-1, keepdims=True)
    o_ref[...] = (x * lax.rsqrt(ms + EPS) * g_ref[...]).astype(o_ref.dtype)


def _rmsnorm(x, g, rows):
    m, d = x.shape
    return pl.pallas_call(
        _rmsnorm_kernel,
        grid=(m // rows,),
        in_specs=[pl.BlockSpec((rows, d), lambda i: (i, 0)),
                  pl.BlockSpec((1, d), lambda i: (0, 0))],
        out_specs=pl.BlockSpec((rows, d), lambda i: (i, 0)),
        out_shape=jax.ShapeDtypeStruct((m, d), BF16),
        compiler_params=pltpu.CompilerParams(dimension_semantics=("parallel",),
                                             vmem_limit_bytes=VMEM_LIMIT),
        name="pre_norm",
    )(x, g.reshape(1, d))


def _postnorm_kernel(x_ref, y_ref, g_ref, o_ref):
    y = y_ref[...]
    ms = jnp.mean(y * y, axis=-1, keepdims=True)
    o_ref[...] = x_ref[...] + y * lax.rsqrt(ms + EPS) * g_ref[...]


def _postnorm_prenorm_kernel(x_ref, y_ref, g_ref, gn_ref, o_ref, h_ref):
    y = y_ref[...]
    ms = jnp.mean(y * y, axis=-1, keepdims=True)
    x = x_ref[...] + y * lax.rsqrt(ms + EPS) * g_ref[...]
    o_ref[...] = x
    ms = jnp.mean(x * x, axis=-1, keepdims=True)
    h_ref[...] = (x * lax.rsqrt(ms + EPS) * gn_ref[...]).astype(h_ref.dtype)


def _postnorm_residual(x, y, g, next_pre_g, rows):
    m, d = x.shape
    row_spec = pl.BlockSpec((rows, d), lambda i: (i, 0))
    gain_spec = pl.BlockSpec((1, d), lambda i: (0, 0))
    params = pltpu.CompilerParams(dimension_semantics=("parallel",), vmem_limit_bytes=VMEM_LIMIT)
    if next_pre_g is None:
        out = pl.pallas_call(
            _postnorm_kernel, grid=(m // rows,),
            in_specs=[row_spec, row_spec, gain_spec], out_specs=row_spec,
            out_shape=jax.ShapeDtypeStruct((m, d), F32),
            compiler_params=params, name="post_norm",
        )(x, y, g.reshape(1, d))
        return out, None
    return pl.pallas_call(
        _postnorm_prenorm_kernel, grid=(m // rows,),
        in_specs=[row_spec, row_spec, gain_spec, gain_spec], out_specs=[row_spec, row_spec],
        out_shape=[jax.ShapeDtypeStruct((m, d), F32), jax.ShapeDtypeStruct((m, d), BF16)],
        compiler_params=params, name="post_pre_norm",
    )(x, y, g.reshape(1, d), next_pre_g.reshape(1, d))


def _matmul_kernel(a_ref, b_ref, o_ref):
    o_ref[...] = jnp.dot(a_ref[...], b_ref[...], preferred_element_type=F32).astype(o_ref.dtype)


def _matmul(a, b_stack, layer, tm, tn, out_dtype, name):
    m, k = a.shape
    _, _, n = b_stack.shape
    return pl.pallas_call(
        _matmul_kernel,
        grid=(m // tm, n // tn),
        in_specs=[pl.BlockSpec((tm, k), lambda i, j: (i, 0)),
                  pl.BlockSpec((None, k, tn), lambda i, j: (layer, 0, j))],
        out_specs=pl.BlockSpec((tm, tn), lambda i, j: (i, j)),
        out_shape=jax.ShapeDtypeStruct((m, n), out_dtype),
        compiler_params=pltpu.CompilerParams(dimension_semantics=("parallel", "parallel"),
                                             vmem_limit_bytes=VMEM_LIMIT),
        name=name,
    )(a, b_stack)


def _matmul_nt_kernel(a_ref, bt_ref, o_ref):
    o_ref[...] = _bdot_nt(a_ref[...], bt_ref[...]).astype(o_ref.dtype)


def _matmul_nt(a, bt_stack, layer, tm, tn, out_dtype, name):
    m, k = a.shape
    _, n, _ = bt_stack.shape
    return pl.pallas_call(
        _matmul_nt_kernel,
        grid=(m // tm, n // tn),
        in_specs=[pl.BlockSpec((tm, k), lambda i, j: (i, 0)),
                  pl.BlockSpec((None, tn, k), lambda i, j: (layer, j, 0))],
        out_specs=pl.BlockSpec((tm, tn), lambda i, j: (i, j)),
        out_shape=jax.ShapeDtypeStruct((m, n), out_dtype),
        compiler_params=pltpu.CompilerParams(dimension_semantics=("parallel", "parallel"),
                                             vmem_limit_bytes=VMEM_LIMIT),
        name=name,
    )(a, bt_stack)


def _cast_tiles_kernel(w_ref, o_ref):
    o_ref[...] = w_ref[...].astype(o_ref.dtype)


def _cast_tiles(w_stack, name):
    depth, k, n = w_stack.shape
    spec = pl.BlockSpec((None, k, W_TILE), lambda l, j: (l, 0, j))
    return pl.pallas_call(
        _cast_tiles_kernel,
        grid=(depth, n // W_TILE),
        in_specs=[spec],
        out_specs=spec,
        out_shape=jax.ShapeDtypeStruct((depth, k, n), BF16),
        compiler_params=pltpu.CompilerParams(dimension_semantics=("parallel", "parallel"),
                                             vmem_limit_bytes=VMEM_LIMIT),
        name=name,
    )(w_stack)


def _cast_row_tiles(wt_stack, n_tiles, src_tile, name):
    depth, _, k = wt_stack.shape
    return pl.pallas_call(
        _cast_tiles_kernel,
        grid=(depth, n_tiles),
        in_specs=[pl.BlockSpec((None, W_TILE, k), lambda l, j: (l, src_tile(j), 0))],
        out_specs=pl.BlockSpec((None, W_TILE, k), lambda l, j: (l, j, 0)),
        out_shape=jax.ShapeDtypeStruct((depth, n_tiles * W_TILE, k), BF16),
        compiler_params=pltpu.CompilerParams(dimension_semantics=("parallel", "parallel"),
                                             vmem_limit_bytes=VMEM_LIMIT),
        name=name,
    )(wt_stack)


def _cast_shifted_rows_kernel(a_ref, b_ref, o_ref):
    o_ref[0:W_TILE - GATE_SHIFT, :] = a_ref[GATE_SHIFT:, :].astype(o_ref.dtype)
    o_ref[W_TILE - GATE_SHIFT:, :] = b_ref[...].astype(o_ref.dtype)


def _cast_gate_rows(wt_stack):
    depth, _, k = wt_stack.shape
    first = SRC_BA // W_TILE
    n_tiles = 2 * D_MODEL // W_TILE
    shifts_per_tile = W_TILE // GATE_SHIFT
    return pl.pallas_call(
        _cast_shifted_rows_kernel,
        grid=(depth, n_tiles),
        in_specs=[pl.BlockSpec((None, W_TILE, k), lambda l, j: (l, first + j, 0)),
                  pl.BlockSpec((None, GATE_SHIFT, k), lambda l, j: (l, (first + j + 1) * shifts_per_tile, 0))],
        out_specs=pl.BlockSpec((None, W_TILE, k), lambda l, j: (l, j, 0)),
        out_shape=jax.ShapeDtypeStruct((depth, n_tiles * W_TILE, k), BF16),
        compiler_params=pltpu.CompilerParams(dimension_semantics=("parallel", "parallel"),
                                             vmem_limit_bytes=VMEM_LIMIT),
        name="cast_gate_weights",
    )(wt_stack, wt_stack)


def _attn_kernel(sinks_ref, q_ref, z_ref, kc_ref, kp_ref, vc_ref, vp_ref, o_ref):
    n = pl.program_id(1)
    blk = q_ref.shape[0]
    qi = lax.broadcasted_iota(jnp.int32, (blk, 2 * blk), 0)
    kj = lax.broadcasted_iota(jnp.int32, (blk, 2 * blk), 1)
    dist = qi + blk - kj
    valid = (dist >= 0) & (dist < WINDOW) & ((kj >= blk) | (n > 0))
    distf = dist.astype(F32)
    kband = jnp.concatenate([kp_ref[...], kc_ref[...]], axis=0).astype(BF16)
    vband = jnp.concatenate([vp_ref[...], vc_ref[...]], axis=0).astype(BF16)
    scale = HD_A ** -0.5
    for h in range(H_KV):
        kh = kband[:, h * HD_A:(h + 1) * HD_A]
        vh = vband[:, h * HD_A:(h + 1) * HD_A]
        heads = [h * GROUP + g for g in range(GROUP)]
        cols = [slice(hh * HD_A, (hh + 1) * HD_A) for hh in heads]
        s = [_bdot_nt(q_ref[:, c], kh) for c in cols]
        p, denom = [], []
        for hh, s_h in zip(heads, s):
            slope = 2.0 ** (-8.0 * (hh + 1) / H_A)
            s_h = jnp.where(valid, s_h * scale - slope * distf, -1e30)
            sink = sinks_ref[hh]
            m = jnp.maximum(jnp.max(s_h, axis=-1, keepdims=True), sink)
            p_h = jnp.exp(s_h - m)
            denom.append(jnp.sum(p_h, axis=-1, keepdims=True) + jnp.exp(sink - m))
            p.append(p_h.astype(BF16))
        o = _bdot(jnp.concatenate(p, axis=0), vh)
        for g, c in enumerate(cols):
            o_h = o[g * blk:(g + 1) * blk] / denom[g]
            o_ref[:, c] = (o_h * _silu(z_ref[:, c])).astype(o_ref.dtype)


def _attention(proj, sinks, batch, seq):
    nb = seq // WINDOW
    k_blk = COL_KA // KV_A
    v_blk = COL_VA // KV_A

    def row(b, n):
        return b * nb + n

    def prev_row(b, n):
        return b * nb + jnp.maximum(n - 1, 0)

    return pl.pallas_call(
        _attn_kernel,
        grid=(batch, nb),
        in_specs=[
            pl.BlockSpec(memory_space=pltpu.SMEM),
            pl.BlockSpec((WINDOW, WIDTH_A), lambda b, n: (row(b, n), COL_QA // WIDTH_A)),
            pl.BlockSpec((WINDOW, WIDTH_A), lambda b, n: (row(b, n), COL_ZA // WIDTH_A)),
            pl.BlockSpec((WINDOW, KV_A), lambda b, n: (row(b, n), k_blk)),
            pl.BlockSpec((WINDOW, KV_A), lambda b, n: (prev_row(b, n), k_blk)),
            pl.BlockSpec((WINDOW, KV_A), lambda b, n: (row(b, n), v_blk)),
            pl.BlockSpec((WINDOW, KV_A), lambda b, n: (prev_row(b, n), v_blk)),
        ],
        out_specs=pl.BlockSpec((WINDOW, WIDTH_A), lambda b, n: (row(b, n), 0)),
        out_shape=jax.ShapeDtypeStruct((batch * seq, WIDTH_A), BF16),
        compiler_params=pltpu.CompilerParams(dimension_semantics=("parallel", "arbitrary"),
                                             vmem_limit_bytes=VMEM_LIMIT),
        name="swa_attention",
    )(sinks, proj, proj, proj, proj, proj, proj)


def _softplus(x):
    return jnp.maximum(x, 0.0) + jnp.log1p(jnp.exp(-jnp.abs(x)))


def _chunk_scans(g, axis):
    size = g.shape[axis]
    pos = lax.broadcasted_iota(jnp.int32, g.shape, axis) % CHUNK
    pre = g
    suf = g
    step = 1
    while step < CHUNK:
        pre = pre + jnp.where(pos >= step, pltpu.roll(pre, step, axis=axis), 0.0)
        suf = suf + jnp.where(pos < CHUNK - step, pltpu.roll(suf, size - step, axis=axis), 0.0)
        step *= 2
    return pre, suf - g


def _gate_kernel(ba_ref, alt_ref, alog_row_ref, dtb_row_ref, alog_col_ref, dtb_col_ref,
                 beta_ref, gc_ref, gtail_ref, gcrow_ref):
    ba = ba_ref[...]
    beta_ref[...] = jax.nn.sigmoid(ba)
    g = -jnp.exp(alog_row_ref[...]) * _softplus(ba + dtb_row_ref[...])
    gc, gtail = _chunk_scans(g, 0)
    gc_ref[...] = gc
    gtail_ref[...] = gtail
    g_row = -jnp.exp(alog_col_ref[...]) * _softplus(alt_ref[...] + dtb_col_ref[...])
    gcrow_ref[...] = _chunk_scans(g_row, 1)[0]


def _gates(proj, alpha_t, a_log, dt_bias):
    rows = proj.shape[0]
    lane_pad = (ALPHA_LANE, 128 - ALPHA_LANE - H_B)
    alog_row = jnp.pad(a_log, lane_pad).reshape(1, 128)
    dtb_row = jnp.pad(dt_bias, lane_pad).reshape(1, 128)
    ba_blk = COL_BA // 128
    col_out = jax.ShapeDtypeStruct((rows, 128), F32)
    col_spec = pl.BlockSpec((SLAB, 128), lambda i: (i, 0))
    small = lambda shape: pl.BlockSpec(shape, lambda i: (0, 0))
    return pl.pallas_call(
        _gate_kernel,
        grid=(rows // SLAB,),
        in_specs=[pl.BlockSpec((SLAB, 128), lambda i: (i, ba_blk)),
                  pl.BlockSpec((H_B, SLAB), lambda i: (0, i)),
                  small((1, 128)), small((1, 128)), small((H_B, 1)), small((H_B, 1))],
        out_specs=[col_spec, col_spec, col_spec, pl.BlockSpec((H_B, SLAB), lambda i: (0, i))],
        out_shape=[col_out, col_out, col_out, jax.ShapeDtypeStruct((H_B, rows), F32)],
        compiler_params=pltpu.CompilerParams(dimension_semantics=("parallel",),
                                             vmem_limit_bytes=VMEM_LIMIT),
        name="gdn_gates",
    )(proj, alpha_t, alog_row, dtb_row, a_log.reshape(H_B, 1), dt_bias.reshape(H_B, 1))


def _unit_lower_inverse(a_list, same16, same32):
    n = a_list[0].shape[0]
    eye = (lax.broadcasted_iota(jnp.int32, (n, n), 0) ==
           lax.broadcasted_iota(jnp.int32, (n, n), 1)).astype(F32)
    d = [jnp.where(same16, a, 0.0) for a in a_list]
    db = [x.astype(BF16) for x in d]
    d2 = [_bdot(x, x).astype(BF16) for x in db]
    t = [eye - x for x in d]
    d4 = [_bdot(x, x).astype(BF16) for x in d2]
    t = [x + _bdot(x, y) for x, y in zip(t, d2)]
    d8 = [_bdot(x, x) for x in d4]
    t = [x + _bdot(x, y) for x, y in zip(t, d4)]
    t = [x + _bdot(x, y) for x, y in zip(t, d8)]
    for lower, upper in ((same16, same32), (same32, None)):
        off = ~lower if upper is None else upper & ~lower
        e = [jnp.where(off, a, 0.0) for a in a_list]
        tb = [x.astype(BF16) for x in t]
        te = [_bdot(x, y) for x, y in zip(tb, e)]
        t = [x - _bdot(y, z) for x, y, z in zip(t, te, tb)]
    return t


def _gdn_kernel(q_ref, k_ref, v_ref, qh_ref, kh_ref, vh_ref, wq_ref, wk_ref, wv_ref,
                beta_ref, gc_ref, gtail_ref, gcrow_ref, z_ref, ng_ref, o_ref,
                state_ref, qx_ref, kx_ref, vx_ref):
    hg = pl.program_id(1)
    t = pl.program_id(2)
    n = q_ref.shape[0]

    @pl.when(t == 0)
    def _():
        state_ref[...] = jnp.zeros_like(state_ref)

    def conv_silu(x_ref, halo_ref, w_ref, xx_ref):
        xx = jnp.concatenate([jnp.where(t > 0, halo_ref[...], 0.0), x_ref[...]], axis=0)
        acc = w_ref[CONV_K - 1:CONV_K, :] * xx[HALO:]
        for back in range(1, CONV_K):
            tap = CONV_K - 1 - back
            acc = acc + w_ref[tap:tap + 1, :] * pltpu.roll(xx, back, axis=0)[HALO:]
        return _silu(acc)

    q_all = conv_silu(q_ref, qh_ref, wq_ref, qx_ref)
    k_all = conv_silu(k_ref, kh_ref, wk_ref, kx_ref)
    v_all = conv_silu(v_ref, vh_ref, wv_ref, vx_ref)

    lane = lax.broadcasted_iota(jnp.int32, (n, 128), 1)

    def pick_lane(ref, idx):
        return jnp.sum(jnp.where(lane == idx, ref[...], 0.0), axis=-1, keepdims=True)

    row = lax.broadcasted_iota(jnp.int32, (n, n), 0)
    col = lax.broadcasted_iota(jnp.int32, (n, n), 1)
    same_chunk = (row // CHUNK) == (col // CHUNK)
    causal = same_chunk & (row >= col)
    strict = same_chunk & (row > col)
    same16 = (row // 16) == (col // 16)
    same32 = (row // 32) == (col // 32)

    heads = range(HEADS_PER_STEP)
    lanes = [slice(hl * DK, (hl + 1) * DK) for hl in heads]
    head = [hg * HEADS_PER_STEP + hl for hl in heads]
    q = [q_all[:, s] for s in lanes]
    k = [k_all[:, s] for s in lanes]
    v = [v_all[:, s] for s in lanes]
    q = [x * (lax.rsqrt(jnp.sum(x * x, axis=-1, keepdims=True) + EPS) * (DK ** -0.5)) for x in q]
    k = [x * lax.rsqrt(jnp.sum(x * x, axis=-1, keepdims=True) + EPS) for x in k]
    beta = [pick_lane(beta_ref, h) for h in head]
    gc = [pick_lane(gc_ref, ALPHA_LANE + h) for h in head]
    gtail = [pick_lane(gtail_ref, ALPHA_LANE + h) for h in head]
    gcrow = [gcrow_ref[pl.ds(h, 1), :] for h in head]
    kb = [x * b for x, b in zip(k, beta)]
    gram = [_bdot_nt(jnp.concatenate([x, y], axis=0), z) for x, y, z in zip(kb, q, k)]
    decay = [jnp.exp(jnp.where(causal, c - r, -1e30)) for c, r in zip(gc, gcrow)]
    a = [jnp.where(strict, g[:n] * d, 0.0) for g, d in zip(gram, decay)]
    attn = [(g[n:] * d).astype(BF16) for g, d in zip(gram, decay)]
    tinv = _unit_lower_inverse(a, same16, same32)
    eg = [jnp.exp(c) for c in gc]
    rhs = [jnp.concatenate([x * b, y * e], axis=1) for x, b, y, e in zip(v, beta, kb, eg)]
    sol = [_bdot(x, y) for x, y in zip(tinv, rhs)]
    u = [x[:, :DV] for x in sol]
    w = [x[:, DV:].astype(BF16) for x in sol]
    qd = [(x * e).astype(BF16) for x, e in zip(q, eg)]
    kt = [(x * jnp.exp(g)).astype(BF16) for x, g in zip(k, gtail)]
    state = [state_ref[hl] for hl in heads]
    v_new = [[] for _ in heads]
    o_state = [[] for _ in heads]
    for c in range(n // CHUNK):
        rows = slice(c * CHUNK, (c + 1) * CHUNK)
        last = (c + 1) * CHUNK - 1
        lhs = [jnp.concatenate([x[rows], y[rows]], axis=0) for x, y in zip(w, qd)]
        ws = [_bdot(x, s) for x, s in zip(lhs, state)]
        vn = [x[rows] - y[:CHUNK] for x, y in zip(u, ws)]
        for hl in heads:
            v_new[hl].append(vn[hl])
            o_state[hl].append(ws[hl][CHUNK:])
        chunk_decay = [jnp.exp(g[last:last + 1, :]) for g in gc]
        state = [s * d + _bdot_tn(x[rows], y) for s, d, x, y in zip(state, chunk_decay, kt, vn)]
    for hl in heads:
        state_ref[hl] = state[hl]
    o = [jnp.concatenate(x, axis=0) + _bdot(y, jnp.concatenate(z, axis=0))
         for x, y, z in zip(o_state, attn, v_new)]
    for hl in heads:
        y = o[hl]
        y = y * lax.rsqrt(jnp.mean(y * y, axis=-1, keepdims=True) + EPS) * ng_ref[...]
        o_ref[:, lanes[hl]] = (y * _silu(z_ref[:, lanes[hl]])).astype(o_ref.dtype)


def _gated_deltanet(proj_main, conv_w, beta, gc, gtail, gcrow, norm_g, batch, seq):
    width = HEADS_PER_STEP * DK
    nt = seq // SLAB
    halo_per_slab = SLAB // HALO

    def rows(b, h, t):
        return b * nt + t

    def halo_rows(b, h, t):
        return jnp.maximum((b * nt + t) * halo_per_slab - 1, 0)

    def sect(col0):
        base = col0 // width
        return (pl.BlockSpec((SLAB, width), lambda b, h, t: (rows(b, h, t), base + h)),
                pl.BlockSpec((HALO, width), lambda b, h, t: (halo_rows(b, h, t), base + h)))

    q_spec, qh_spec = sect(COL_QB)
    k_spec, kh_spec = sect(COL_KB)
    v_spec, vh_spec = sect(COL_VB)
    z_spec, _ = sect(COL_ZB)

    def conv_spec(col0):
        base = col0 // width
        return pl.BlockSpec((CONV_K, width), lambda b, h, t: (0, base + h))

    col_spec = pl.BlockSpec((SLAB, 128), lambda b, h, t: (rows(b, h, t), 0))
    return pl.pallas_call(
        _gdn_kernel,
        grid=(batch, H_B // HEADS_PER_STEP, nt),
        in_specs=[q_spec, k_spec, v_spec, qh_spec, kh_spec, vh_spec,
                  conv_spec(0), conv_spec(H_B * DK), conv_spec(2 * H_B * DK),
                  col_spec, col_spec, col_spec,
                  pl.BlockSpec((H_B, SLAB), lambda b, h, t: (0, rows(b, h, t))),
                  z_spec,
                  pl.BlockSpec((1, DV), lambda b, h, t: (0, 0))],
        out_specs=pl.BlockSpec((SLAB, width), lambda b, h, t: (rows(b, h, t), h)),
        out_shape=jax.ShapeDtypeStruct((batch * seq, WIDTH_B), BF16),
        scratch_shapes=[pltpu.VMEM((HEADS_PER_STEP, DK, DV), F32),
                        pltpu.VMEM((HALO + SLAB, width), F32),
                        pltpu.VMEM((HALO + SLAB, width), F32),
                        pltpu.VMEM((HALO + SLAB, width), F32)],
        compiler_params=pltpu.CompilerParams(
            dimension_semantics=("parallel", "parallel", "arbitrary"),
            vmem_limit_bytes=VMEM_LIMIT),
        name="gated_deltanet",
    )(proj_main, proj_main, proj_main, proj_main, proj_main, proj_main,
      conv_w, conv_w, conv_w, beta, gc, gtail, gcrow, proj_main, norm_g.reshape(1, DV))


def _merge_kernel(oa_ref, ob_ref, wa_ref, wb_ref, ga_ref, gb_ref, o_ref):
    ya = jnp.dot(oa_ref[...], wa_ref[...], preferred_element_type=F32)
    yb = jnp.dot(ob_ref[...], wb_ref[...], preferred_element_type=F32)
    merged = jax.nn.sigmoid(ga_ref[...]) * ya + jax.nn.sigmoid(gb_ref[...]) * yb
    o_ref[...] = merged.astype(o_ref.dtype)


def _merge(o_a, o_b, w_pa, w_pb, layer, gates, tm, tn):
    m = o_a.shape[0]
    gb_blk = D_MODEL // tn
    return pl.pallas_call(
        _merge_kernel,
        grid=(m // tm, D_MODEL // tn),
        in_specs=[pl.BlockSpec((tm, WIDTH_A), lambda i, j: (i, 0)),
                  pl.BlockSpec((tm, WIDTH_B), lambda i, j: (i, 0)),
                  pl.BlockSpec((None, WIDTH_A, tn), lambda i, j: (layer, 0, j)),
                  pl.BlockSpec((None, WIDTH_B, tn), lambda i, j: (layer, 0, j)),
                  pl.BlockSpec((tm, tn), lambda i, j: (i, j)),
                  pl.BlockSpec((tm, tn), lambda i, j: (i, gb_blk + j))],
        out_specs=pl.BlockSpec((tm, tn), lambda i, j: (i, j)),
        out_shape=jax.ShapeDtypeStruct((m, D_MODEL), BF16),
        compiler_params=pltpu.CompilerParams(dimension_semantics=("parallel", "parallel"),
                                             vmem_limit_bytes=VMEM_LIMIT),
        name="branch_merge",
    )(o_a, o_b, w_pa, w_pb, gates, gates)


def _main_src_tile(j):
    last = COL_KA // W_TILE
    return jnp.where(j < SRC_KV_TILE, j, jnp.where(j < last, j + 1, jnp.where(j == last, SRC_KV_TILE, j)))


def _layer(x2, h, batch, seq, layer, wb_main, wb_gates, sinks, conv_w, a_log, dt_bias, gdn_g,
           wb_pa, wb_pb, wb_o, post_g, next_pre_g):
    proj = _matmul_nt(h, wb_main, layer, 1024, 1024, F32, "in_proj_main")
    gates = _matmul_nt(h, wb_gates, layer, 1024, 1024, F32, "in_proj_gates")
    o_a = _attention(proj, sinks, batch, seq)
    alpha_t = proj[:, COL_BA + ALPHA_LANE:COL_BA + ALPHA_LANE + H_B].T
    beta, gc, gtail, gcrow = _gates(proj, alpha_t, a_log, dt_bias)
    o_b = _gated_deltanet(proj, conv_w, beta, gc, gtail, gcrow, gdn_g, batch, seq)
    merged = _merge(o_a, o_b, wb_pa, wb_pb, layer, gates, 1024, 512)
    out = _matmul(merged, wb_o, layer, 1024, 1024, F32, "out_proj")
    return _postnorm_residual(x2, out, post_g, next_pre_g, 256)


def kernel(x, pre_norm_g, w_in, sinks, conv_w, a_log, dt_bias, gdn_norm_g, w_pa, w_pb, w_o, post_norm_g):
    batch, seq, d = x.shape
    depth = w_in.shape[0]
    x2 = x.reshape(batch * seq, d)
    w_in_t = jnp.transpose(w_in, (0, 2, 1))
    wb_main = _cast_row_tiles(w_in_t, N_MAIN // W_TILE, _main_src_tile, "cast_in_weights")
    wb_gates = _cast_gate_rows(w_in_t)
    wb_pa = _cast_tiles(w_pa, "cast_pa_weights")
    wb_pb = _cast_tiles(w_pb, "cast_pb_weights")
    wb_o = _cast_tiles(w_o, "cast_out_weights")
    h = _rmsnorm(x2, pre_norm_g[0], 256)
    for layer in range(depth):
        next_pre_g = pre_norm_g[layer + 1] if layer + 1 < depth else None
        x2, h = _layer(x2, h, batch, seq, layer, wb_main, wb_gates, sinks[layer], conv_w[layer],
                       a_log[layer], dt_bias[layer], gdn_norm_g[layer], wb_pa, wb_pb, wb_o,
                       post_norm_g[layer], next_pre_g)
    return x2.reshape(batch, seq, d)
```
